```python
import math
import jax, jax.numpy as jnp
from jax import lax
import numpy as np

D_MODEL = 1024
BATCH = 32
SEQ = 2048
DEPTH = 2

CHUNK = 64
Q_BLOCK = 128
HEAD_DIM = 64
N_HEADS = D_MODEL // HEAD_DIM
N_GROUPS = 4
HEADS_PER_GROUP = N_HEADS // N_GROUPS
GROUP_WIDTH = HEADS_PER_GROUP * HEAD_DIM
MIX_WIDTH = N_GROUPS * GROUP_WIDTH
BAND_CHUNKS = 8
BAND = (BAND_CHUNKS + 1) * CHUNK
MAX_REL_DIST = 256
N_REL = 2 * MAX_REL_DIST + 1
DIFF_QK_DIM = HEAD_DIM // 2
D_FF = ((8 * D_MODEL // 3 + 255) // 256) * 256
FFN_RES = 0.5
RMS_EPS = 1e-6
GROUP_SPLIT_SIZES = (GROUP_WIDTH, GROUP_WIDTH, GROUP_WIDTH, HEADS_PER_GROUP,
                     GROUP_WIDTH, GROUP_WIDTH, GROUP_WIDTH,
                     GROUP_WIDTH, GROUP_WIDTH, GROUP_WIDTH,
                     GROUP_WIDTH, GROUP_WIDTH, GROUP_WIDTH)
IN_WIDTH = 12 * GROUP_WIDTH + HEADS_PER_GROUP

kernel_name = 'hybrid_chunk_causal_parallel_heads'


def _split_points():
    return [int(p) for p in np.cumsum(GROUP_SPLIT_SIZES)[:-1]]


def rmsnorm(x, g):
    xf = x.astype(jnp.float32)
    y = xf * lax.rsqrt(jnp.mean(xf * xf, axis=-1, keepdims=True) + RMS_EPS)
    return (y * g.astype(jnp.float32)).astype(x.dtype)


def swiglu(h, w_gu, w_down):
    gate, up = jnp.split(h @ w_gu, 2, axis=-1)
    return (jax.nn.silu(gate) * up) @ w_down


def to_heads(t):
    b, s, w = t.shape
    return t.reshape(b, s, w // HEAD_DIM, HEAD_DIM).transpose(0, 2, 1, 3)


def from_heads(t):
    b, h, s, d = t.shape
    return t.transpose(0, 2, 1, 3).reshape(b, s, h * d)


def alibi_slopes(n):
    return jnp.asarray([2.0 ** (-8.0 * (i + 1) / n) for i in range(n)], dtype=jnp.float32)


def lambda_init(layer_idx):
    return 0.8 - 0.6 * math.exp(-0.3 * layer_idx)


def forgetting_attention(q, k, v, log_f):
    seq, d = q.shape[2], q.shape[3]
    scale = d ** -0.5
    cum_f = jnp.cumsum(log_f, axis=-1)
    outs = []
    for i in range(seq // Q_BLOCK):
        q0, q1 = i * Q_BLOCK, (i + 1) * Q_BLOCK
        logits = jnp.einsum('bhqd,bhkd->bhqk', q[:, :, q0:q1], k[:, :, :q1]).astype(jnp.float32) * scale
        logits = logits + cum_f[:, :, q0:q1, None] - cum_f[:, :, None, :q1]
        t = jnp.arange(q0, q1)[:, None]
        s = jnp.arange(q1)[None, :]
        logits = jnp.where(s <= t, logits, -jnp.inf)
        p = jax.nn.softmax(logits, axis=-1)
        outs.append(jnp.einsum('bhqk,bhkd->bhqd', p.astype(v.dtype), v[:, :, :q1]))
    return jnp.concatenate(outs, axis=2)


def chunked_relpos_attention(q, k, v, rel_table):
    b, h, seq, d = q.shape
    scale = d ** -0.5
    n_chunks = seq // CHUNK
    pad = BAND_CHUNKS * CHUNK
    kp = jnp.pad(k, ((0, 0), (0, 0), (pad, 0), (0, 0)))
    vp = jnp.pad(v, ((0, 0), (0, 0), (pad, 0), (0, 0)))
    rel = pad + np.arange(CHUNK)[:, None] - np.arange(BAND)[None, :]
    rel_idx = np.clip(rel, -MAX_REL_DIST, MAX_REL_DIST) + MAX_REL_DIST
    bias = rel_table.astype(jnp.float32)[:, rel_idx]

    def chunk_fn(c):
        qc = lax.dynamic_slice_in_dim(q, c * CHUNK, CHUNK, axis=2)
        kc = lax.dynamic_slice_in_dim(kp, c * CHUNK, BAND, axis=2)
        vc = lax.dynamic_slice_in_dim(vp, c * CHUNK, BAND, axis=2)
        logits = jnp.einsum('bhqd,bhkd->bhqk', qc, kc).astype(jnp.float32) * scale + bias
        key_pos = c * CHUNK - pad + jnp.arange(BAND)
        logits = jnp.where(key_pos >= 0, logits, -jnp.inf)
        p = jax.nn.softmax(logits, axis=-1)
        return jnp.einsum('bhqk,bhkd->bhqd', p.astype(v.dtype), vc)

    out = lax.map(chunk_fn, jnp.arange(n_chunks))
    return out.transpose(1, 2, 0, 3, 4).reshape(b, h, seq, d)


def diff_attention(q, k, v, lam_params, lam_init):
    seq = q.shape[2]
    q1, q2 = q[..., :DIFF_QK_DIM], q[..., DIFF_QK_DIM:]
    k1, k2 = k[..., :DIFF_QK_DIM], k[..., DIFF_QK_DIM:]
    lp = lam_params.astype(jnp.float32)
    lam = jnp.exp(jnp.sum(lp[0] * lp[1])) - jnp.exp(jnp.sum(lp[2] * lp[3])) + lam_init
    scale = DIFF_QK_DIM ** -0.5
    slopes = alibi_slopes(q.shape[1])[:, None, None]
    outs = []
    for i in range(seq // Q_BLOCK):
        q0, q1e = i * Q_BLOCK, (i + 1) * Q_BLOCK
        t = jnp.arange(q0, q1e)[:, None]
        s = jnp.arange(q1e)[None, :]
        alibi = -slopes * jnp.abs(t - s).astype(jnp.float32)
        mask = (s // CHUNK) <= (t // CHUNK)
        l1 = jnp.einsum('bhqd,bhkd->bhqk', q1[:, :, q0:q1e], k1[:, :, :q1e]).astype(jnp.float32) * scale + alibi
        l2 = jnp.einsum('bhqd,bhkd->bhqk', q2[:, :, q0:q1e], k2[:, :, :q1e]).astype(jnp.float32) * scale + alibi
        p = (jax.nn.softmax(jnp.where(mask, l1, -jnp.inf), axis=-1)
             - lam * jax.nn.softmax(jnp.where(mask, l2, -jnp.inf), axis=-1))
        outs.append(jnp.einsum('bhqk,bhkd->bhqd', p.astype(v.dtype), v[:, :, :q1e]))
    o = jnp.concatenate(outs, axis=2).astype(jnp.float32)
    o = o * lax.rsqrt(jnp.mean(o * o, axis=-1, keepdims=True) + RMS_EPS) * (1.0 - lam_init)
    return o.astype(v.dtype)


def stick_breaking_attention(q, k, v):
    seq, d = q.shape[2], q.shape[3]
    scale = d ** -0.5
    outs = []
    for i in range(seq // Q_BLOCK):
        q0, q1 = i * Q_BLOCK, (i + 1) * Q_BLOCK
        z = jnp.einsum('bhqd,bhkd->bhqk', q[:, :, q0:q1], k[:, :, :q1]).astype(jnp.float32) * scale
        t = jnp.arange(q0, q1)[:, None]
        s = jnp.arange(q1)[None, :]
        strict = s < t
        log_one_minus = jnp.where(strict, jax.nn.log_sigmoid(-z), 0.0)
        suffix = lax.cumsum(log_one_minus, axis=3, reverse=True) - log_one_minus
        weights = jnp.where(strict, jnp.exp(jax.nn.log_sigmoid(z) + suffix), 0.0)
        outs.append(jnp.einsum('bhqk,bhkd->bhqd', weights.astype(v.dtype), v[:, :, :q1]))
    return jnp.concatenate(outs, axis=2)


def hybrid_mixer(h, w_in, b_f, rel_table, lam_params, w_out, lam_init):
    proj = h @ w_in
    (qa, ka, va, fa, qb, kb, vb, qc, kc, vc, qd, kd, vd) = jnp.split(proj, _split_points(), axis=-1)
    log_f = jax.nn.log_sigmoid((fa + b_f).astype(jnp.float32)).transpose(0, 2, 1)
    o_a = forgetting_attention(to_heads(qa), to_heads(ka), to_heads(va), log_f)
    o_b = chunked_relpos_attention(to_heads(qb), to_heads(kb), to_heads(vb), rel_table)
    o_c = diff_attention(to_heads(qc), to_heads(kc), to_heads(vc), lam_params, lam_init)
    o_d = stick_breaking_attention(to_heads(qd), to_heads(kd), to_heads(vd))
    mixed = jnp.concatenate([from_heads(o_a), from_heads(o_b), from_heads(o_c), from_heads(o_d)], axis=-1)
    return mixed @ w_out


def setup_inputs(seed: int = 0) -> dict:
    key = jax.random.key(seed)
    ks = jax.random.split(key, 16)
    f32 = jnp.float32
    x = jax.random.normal(ks[0], (BATCH, SEQ, D_MODEL), f32)
    g_ffn1 = 1.0 + 0.02 * jax.random.normal(ks[1], (DEPTH, D_MODEL), f32)
    ffn1_w_gu = jax.random.normal(ks[2], (DEPTH, D_MODEL, 2 * D_FF), f32) * D_MODEL ** -0.5
    ffn1_w_down = jax.random.normal(ks[3], (DEPTH, D_FF, D_MODEL), f32) * D_FF ** -0.5
    g_mix = 1.0 + 0.02 * jax.random.normal(ks[4], (DEPTH, D_MODEL), f32)
    w_in = jax.random.normal(ks[5], (DEPTH, D_MODEL, IN_WIDTH), f32) * D_MODEL ** -0.5
    b_f = 0.1 * jax.random.normal(ks[6], (DEPTH, HEADS_PER_GROUP), f32)
    rel_bias = 0.2 * jax.random.normal(ks[7], (DEPTH, HEADS_PER_GROUP, N_REL), f32)
    diff_lambda = 0.1 * jax.random.normal(ks[8], (DEPTH, 4, DIFF_QK_DIM), f32)
    w_out = jax.random.normal(ks[9], (DEPTH, MIX_WIDTH, D_MODEL), f32) * MIX_WIDTH ** -0.5
    g_ffn2 = 1.0 + 0.02 * jax.random.normal(ks[10], (DEPTH, D_MODEL), f32)
    ffn2_w_gu = jax.random.normal(ks[11], (DEPTH, D_MODEL, 2 * D_FF), f32) * D_MODEL ** -0.5
    ffn2_w_down = jax.random.normal(ks[12], (DEPTH, D_FF, D_MODEL), f32) * D_FF ** -0.5
    g_final = 1.0 + 0.02 * jax.random.normal(ks[13], (D_MODEL,), f32)
    return {'x': x, 'g_ffn1': g_ffn1, 'ffn1_w_gu': ffn1_w_gu, 'ffn1_w_down': ffn1_w_down,
            'g_mix': g_mix, 'w_in': w_in, 'b_f': b_f, 'rel_bias': rel_bias,
            'diff_lambda': diff_lambda, 'w_out': w_out, 'g_ffn2': g_ffn2,
            'ffn2_w_gu': ffn2_w_gu, 'ffn2_w_down': ffn2_w_down, 'g_final': g_final}


def reference(x, g_ffn1, ffn1_w_gu, ffn1_w_down, g_mix, w_in, b_f, rel_bias, diff_lambda,
              w_out, g_ffn2, ffn2_w_gu, ffn2_w_down, g_final):
    for l in range(DEPTH):
        x = x + FFN_RES * swiglu(rmsnorm(x, g_ffn1[l]), ffn1_w_gu[l], ffn1_w_down[l])
        x = x + hybrid_mixer(rmsnorm(x, g_mix[l]), w_in[l], b_f[l], rel_bias[l], diff_lambda[l],
                             w_out[l], lambda_init(l))
        x = x + FFN_RES * swiglu(rmsnorm(x, g_ffn2[l]), ffn2_w_gu[l], ffn2_w_down[l])
    return rmsnorm(x, g_final)
```

```python
import functools
import math

import numpy as np
import jax
import jax.numpy as jnp
from jax import lax
from jax.experimental import pallas as pl
from jax.experimental.pallas import tpu as pltpu

F32 = jnp.float32
BF16 = jnp.bfloat16

LANES = 128
HEAD_DIM = 64
HEADS_PER_GROUP = 4
GROUP_WIDTH = HEADS_PER_GROUP * HEAD_DIM
N_GROUPS = 4
CHUNK = 64
BAND_CHUNKS = 8
MAX_REL_DIST = 256
DIFF_QK_DIM = HEAD_DIM // 2
FFN_RES = 0.5
RMS_EPS = 1e-6
NEG = -1e30

TOKEN_TILE = 512
ATTN_TILE = 256
FFN_CHUNK = 1408
VMEM_LIMIT = 56 * 1024 * 1024

_NT = (((1,), (1,)), ((), ()))


def _rmsnorm(x, g):
    y = x * lax.rsqrt(jnp.mean(x * x, axis=-1, keepdims=True) + RMS_EPS)
    return y * g


def _log_sigmoid(x):
    return jnp.minimum(x, 0.0) - jnp.log1p(jnp.exp(-jnp.abs(x)))


def _split3(x):
    h1 = x.astype(BF16)
    r1 = x - h1.astype(F32)
    h2 = r1.astype(BF16)
    h3 = (r1 - h2.astype(F32)).astype(BF16)
    return h1, h2, h3


def _params(sem):
    return pltpu.CompilerParams(dimension_semantics=sem, vmem_limit_bytes=VMEM_LIMIT)


def _resident(shape):
    zeros = (0,) * len(shape)
    return pl.BlockSpec(shape, lambda *_: zeros, pipeline_mode=pl.Buffered(1))


def _ffn_kernel(x_ref, g_ref, wgu_ref, wd_ref, o_ref, *, d_ff, chunk):
    x = x_ref[...]
    h = _rmsnorm(x, g_ref[...]).astype(BF16)
    acc = jnp.zeros(x.shape, F32)
    for c0 in range(0, d_ff, chunk):
        gate = jnp.dot(h, wgu_ref[:, c0:c0 + chunk], preferred_element_type=F32)
        up = jnp.dot(h, wgu_ref[:, d_ff + c0:d_ff + c0 + chunk], preferred_element_type=F32)
        act = gate * jax.nn.sigmoid(gate) * up
        acc = acc + jnp.dot(act.astype(BF16), wd_ref[c0:c0 + chunk, :], preferred_element_type=F32)
    o_ref[...] = x + FFN_RES * acc


def _ffn(x2, g, w_gu, w_down):
    n, d = x2.shape
    d_ff = w_down.shape[0]
    chunk = FFN_CHUNK if d_ff % FFN_CHUNK == 0 else d_ff
    tm = min(TOKEN_TILE, n)
    return pl.pallas_call(
        functools.partial(_ffn_kernel, d_ff=d_ff, chunk=chunk),
        grid=(n // tm,),
        in_specs=[
            pl.BlockSpec((tm, d), lambda i: (i, 0)),
            _resident((1, d)),
            _resident((d, 2 * d_ff)),
            _resident((d_ff, d)),
        ],
        out_specs=pl.BlockSpec((tm, d), lambda i: (i, 0)),
        out_shape=jax.ShapeDtypeStruct((n, d), F32),
        compiler_params=_params(("arbitrary",)),
        name="ffn",
    )(x2, g.reshape(1, d), w_gu.astype(BF16), w_down.astype(BF16))


def _inproj_kernel(x_ref, g_ref, w_ref, cs_ref, wf_ref, wft_ref, bcol_ref, brow_ref,
                   proj_ref, fcol_ref, frow_ref, ccol_ref, crow_ref, *, tm):
    @pl.when(pl.program_id(1) == 0)
    def _():
        ccol_ref[...] = jnp.zeros_like(ccol_ref)
        crow_ref[...] = jnp.zeros_like(crow_ref)

    h = _rmsnorm(x_ref[0], g_ref[...]).astype(BF16)
    proj = jnp.dot(h, w_ref[...], preferred_element_type=F32)
    proj_ref[0] = (proj * cs_ref[...]).astype(BF16)

    ri = lax.broadcasted_iota(jnp.int32, (tm, tm), 0)
    ci = lax.broadcasted_iota(jnp.int32, (tm, tm), 1)
    tril = jnp.where(ci <= ri, 1.0, 0.0).astype(BF16)

    ls_col = _log_sigmoid(jnp.dot(h, wf_ref[...], preferred_element_type=F32) + bcol_ref[...])
    cum_col = ccol_ref[...]
    for part in _split3(ls_col):
        cum_col = cum_col + jnp.dot(tril, part, preferred_element_type=F32)
    fcol_ref[0] = cum_col
    ccol_ref[...] = cum_col[tm - 1:tm, :]

    ls_row = _log_sigmoid(
        lax.dot_general(wft_ref[...], h, _NT, preferred_element_type=F32) + brow_ref[...])
    cum_row = crow_ref[...]
    for part in _split3(ls_row):
        cum_row = cum_row + lax.dot_general(part, tril, _NT, preferred_element_type=F32)
    frow_ref[0] = cum_row
    crow_ref[...] = cum_row[:, tm - 1:tm]


def _inproj(x3, g, w_in, b_f):
    b, s, d = x3.shape
    gw = GROUP_WIDTH
    nf = HEADS_PER_GROUP
    w_main = jnp.concatenate([w_in[:, :3 * gw], w_in[:, 3 * gw + nf:]], axis=1).astype(BF16)
    w_f = w_in[:, 3 * gw:3 * gw + nf]
    wf_col = jnp.pad(w_f, ((0, 0), (0, LANES - nf))).astype(BF16)
    wf_row = jnp.pad(w_f.T, ((0, 8 - nf), (0, 0))).astype(BF16)
    b_col = jnp.pad(b_f, (0, LANES - nf)).reshape(1, LANES)
    b_row = jnp.pad(b_f, (0, 8 - nf)).reshape(8, 1)
    width = w_main.shape[1]
    cs = np.ones((1, width), np.float32)
    for grp in range(N_GROUPS):
        scale = DIFF_QK_DIM ** -0.5 if grp == 2 else HEAD_DIM ** -0.5
        cs[0, 3 * grp * gw:(3 * grp + 1) * gw] = scale
    tm = min(TOKEN_TILE, s)
    return pl.pallas_call(
        functools.partial(_inproj_kernel, tm=tm),
        grid=(b, s // tm),
        in_specs=[
            pl.BlockSpec((1, tm, d), lambda bi, j: (bi, j, 0)),
            _resident((1, d)),
            _resident((d, width)),
            _resident((1, width)),
            _resident((d, LANES)),
            _resident((8, d)),
            _resident((1, LANES)),
            _resident((8, 1)),
        ],
        out_specs=[
            pl.BlockSpec((1, tm, width), lambda bi, j: (bi, j, 0)),
            pl.BlockSpec((1, tm, LANES), lambda bi, j: (bi, j, 0)),
            pl.BlockSpec((1, 8, tm), lambda bi, j: (bi, 0, j)),
        ],
        out_shape=[
            jax.ShapeDtypeStruct((b, s, width), BF16),
            jax.ShapeDtypeStruct((b, s, LANES), F32),
            jax.ShapeDtypeStruct((b, 8, s), F32),
        ],
        scratch_shapes=[pltpu.VMEM((1, LANES), F32), pltpu.VMEM((8, 1), F32)],
        compiler_params=_params(("arbitrary", "arbitrary")),
        name="inproj",
    )(x3, g.reshape(1, d), w_main, jnp.asarray(cs), wf_col, wf_row, b_col, b_row)


def _head_lanes(hh):
    lane = lax.broadcasted_iota(jnp.int32, (1, LANES), 1)
    return (lane >= hh * HEAD_DIM) & (lane < (hh + 1) * HEAD_DIM)


def _softmax_step(carry, s, v):
    m, l, acc = carry
    m_new = jnp.maximum(m, jnp.max(s, axis=-1, keepdims=True))
    alpha = jnp.exp(m - m_new)
    p = jnp.exp(s - m_new)
    l = alpha * l + jnp.sum(p, axis=-1, keepdims=True)
    acc = alpha * acc + jnp.dot(p.astype(BF16), v, preferred_element_type=F32)
    return m_new, l, acc


def _softmax_init(tq):
    return (jnp.full((tq, 1), NEG, F32), jnp.zeros((tq, 1), F32), jnp.zeros((tq, LANES), F32))


def _kv_tile(k_ref, v_ref, kb, pair, t):
    k0 = pl.multiple_of(kb * t, t)
    cols = slice(pair * LANES, (pair + 1) * LANES)
    return k0, k_ref[0, pl.ds(k0, t), cols], v_ref[0, pl.ds(k0, t), cols]


def _fox_kernel(q_ref, k_ref, v_ref, fcol_ref, frow_ref, o_ref, *, t):
    qi = pl.program_id(1)
    row = lax.broadcasted_iota(jnp.int32, (t, t), 0)
    col = lax.broadcasted_iota(jnp.int32, (t, t), 1)
    causal = col <= row
    for pair in range(2):
        q = q_ref[0, :, pair * LANES:(pair + 1) * LANES]
        out = None
        for hh in range(2):
            head = 2 * pair + hh
            qh = jnp.where(_head_lanes(hh), q, jnp.zeros_like(q))
            f_t = fcol_ref[0, :, head:head + 1]

            def step(kb, carry, masked, pair=pair, head=head, qh=qh, f_t=f_t):
                k0, k, v = _kv_tile(k_ref, v_ref, kb, pair, t)
                f_s = frow_ref[0, head:head + 1, pl.ds(k0, t)]
                s = lax.dot_general(qh, k, _NT, preferred_element_type=F32) + f_t - f_s
                if masked:
                    s = jnp.where(causal, s, NEG)
                return _softmax_step(carry, s, v)

            carry = lax.fori_loop(0, qi, functools.partial(step, masked=False), _softmax_init(t))
            _, l, acc = step(qi, carry, True)
            o_h = acc * (1.0 / l)
            out = o_h if hh == 0 else jnp.where(_head_lanes(0), out, o_h)
        o_ref[0, :, pair * LANES:(pair + 1) * LANES] = out.astype(o_ref.dtype)


def _band_kernel(q_ref, k_ref, v_ref, bias_ref, o_ref, *, t, reach):
    qi = pl.program_id(1)
    kb_lo = jnp.maximum(qi - reach, 0)
    for pair in range(2):
        q = q_ref[0, :, pair * LANES:(pair + 1) * LANES]
        out = None
        for hh in range(2):
            head = 2 * pair + hh
            qh = jnp.where(_head_lanes(hh), q, jnp.zeros_like(q))

            def step(kb, carry, pair=pair, head=head, qh=qh):
                _, k, v = _kv_tile(k_ref, v_ref, kb, pair, t)
                bias = bias_ref[head, pl.ds(kb - qi + reach, 1)][0]
                s = lax.dot_general(qh, k, _NT, preferred_element_type=F32) + bias
                return _softmax_step(carry, s, v)

            _, l, acc = lax.fori_loop(kb_lo, qi + 1, step, _softmax_init(t))
            o_h = acc * (1.0 / l)
            out = o_h if hh == 0 else jnp.where(_head_lanes(0), out, o_h)
        o_ref[0, :, pair * LANES:(pair + 1) * LANES] = out.astype(o_ref.dtype)


def _diff_kernel(q_ref, k_ref, v_ref, lam_ref, o_ref, *, t, lam_init):
    qi = pl.program_id(1)
    row = lax.broadcasted_iota(jnp.int32, (t, t), 0)
    col = lax.broadcasted_iota(jnp.int32, (t, t), 1)
    visible = (col // CHUNK) <= (row // CHUNK)
    dist = (row - col).astype(F32)
    lane = lax.broadcasted_iota(jnp.int32, (1, LANES), 1)
    lp = lam_ref[...]
    lam = (jnp.exp(jnp.sum(lp[0:1] * lp[1:2], axis=-1, keepdims=True))
           - jnp.exp(jnp.sum(lp[2:3] * lp[3:4], axis=-1, keepdims=True)) + lam_init)
    for pair in range(2):
        q = q_ref[0, :, pair * LANES:(pair + 1) * LANES]
        out = None
        for hh in range(2):
            head = 2 * pair + hh
            slope = 2.0 ** (-8.0 * (head + 1) / HEADS_PER_GROUP)
            rel_off = dist * (-slope)
            rel_diag = jnp.where(visible, jnp.abs(dist) * (-slope), NEG)
            probs = []
            for mp in range(2):
                lo = hh * HEAD_DIM + mp * DIFF_QK_DIM
                qm = jnp.where((lane >= lo) & (lane < lo + DIFF_QK_DIM), q, jnp.zeros_like(q))

                def step(kb, carry, pair=pair, qm=qm, rel_off=rel_off, slope=slope):
                    _, k, v = _kv_tile(k_ref, v_ref, kb, pair, t)
                    shift = (-slope * t) * (qi - kb).astype(F32)
                    s = lax.dot_general(qm, k, _NT, preferred_element_type=F32) + rel_off + shift
                    return _softmax_step(carry, s, v)

                carry = lax.fori_loop(0, qi, step, _softmax_init(t))
                _, k, v = _kv_tile(k_ref, v_ref, qi, pair, t)
                s = lax.dot_general(qm, k, _NT, preferred_element_type=F32) + rel_diag
                _, l, acc = _softmax_step(carry, s, v)
                probs.append(acc * (1.0 / l))
            o_h = probs[0] - lam * probs[1]
            in_head = _head_lanes(hh)
            ms = jnp.sum(jnp.where(in_head, o_h * o_h, 0.0), axis=-1, keepdims=True) * (1.0 / HEAD_DIM)
            o_h = o_h * lax.rsqrt(ms + RMS_EPS) * (1.0 - lam_init)
            out = o_h if hh == 0 else jnp.where(_head_lanes(0), out, o_h)
        o_ref[0, :, pair * LANES:(pair + 1) * LANES] = out.astype(o_ref.dtype)


def _stick_kernel(q_ref, k_ref, v_ref, o_ref, *, t):
    qi = pl.program_id(1)
    row = lax.broadcasted_iota(jnp.int32, (t, t), 0)
    col = lax.broadcasted_iota(jnp.int32, (t, t), 1)
    strict = col < row
    suffix_ones = jnp.where(row >= col, 1.0, 0.0).astype(BF16)
    for pair in range(2):
        q = q_ref[0, :, pair * LANES:(pair + 1) * LANES]
        out = None
        for hh in range(2):
            qh = jnp.where(_head_lanes(hh), q, jnp.zeros_like(q))

            def step(kb, carry, masked, pair=pair, qh=qh):
                tail, acc = carry
                _, k, v = _kv_tile(k_ref, v_ref, kb, pair, t)
                z = lax.dot_general(qh, k, _NT, preferred_element_type=F32)
                log_rest = _log_sigmoid(-z)
                if masked:
                    log_rest = jnp.where(strict, log_rest, 0.0)
                hi = log_rest.astype(BF16)
                lo = (log_rest - hi.astype(F32)).astype(BF16)
                within = (jnp.dot(hi, suffix_ones, preferred_element_type=F32)
                          + jnp.dot(lo, suffix_ones, preferred_element_type=F32))
                w = jnp.exp(z + within + tail)
                if masked:
                    w = jnp.where(strict, w, 0.0)
                acc = acc + jnp.dot(w.astype(BF16), v, preferred_element_type=F32)
                return tail + within[:, 0:1], acc

            carry = step(qi, (jnp.zeros((t, 1), F32), jnp.zeros((t, LANES), F32)), True)
            _, o_h = lax.fori_loop(
                0, qi, lambda it, c: step(qi - 1 - it, c, False), carry)
            out = o_h if hh == 0 else jnp.where(_head_lanes(0), out, o_h)
        o_ref[0, :, pair * LANES:(pair + 1) * LANES] = out.astype(o_ref.dtype)


def _mixer_call(body, proj, group, extra_inputs, extra_specs, name):
    b, s, _ = proj.shape
    t = min(ATTN_TILE, s)
    gw = GROUP_WIDTH
    return pl.pallas_call(
        functools.partial(body, t=t),
        grid=(b, s // t),
        in_specs=[
            pl.BlockSpec((1, t, gw), lambda bi, qi: (bi, qi, 3 * group)),
            pl.BlockSpec((1, s, gw), lambda bi, qi: (bi, 0, 3 * group + 1)),
            pl.BlockSpec((1, s, gw), lambda bi, qi: (bi, 0, 3 * group + 2)),
        ] + extra_specs,
        out_specs=pl.BlockSpec((1, t, gw), lambda bi, qi: (bi, qi, 0)),
        out_shape=jax.ShapeDtypeStruct((b, s, gw), BF16),
        compiler_params=_params(("arbitrary", "arbitrary")),
        name=name,
    )(proj, proj, proj, *extra_inputs)


def _band_bias(rel_table, t, reach):
    i = np.arange(t)[:, None]
    r = np.arange((reach + 1) * t)[None, :]
    rel = reach * t + i - r
    idx = np.clip(rel, -MAX_REL_DIST, MAX_REL_DIST) + MAX_REL_DIST
    kc = r // CHUNK - (reach * t // CHUNK - BAND_CHUNKS)
    qc = i // CHUNK
    visible = (kc >= qc) & (kc <= qc + BAND_CHUNKS)
    bias = jnp.where(visible[None], rel_table.astype(F32)[:, idx], NEG)
    h = rel_table.shape[0]
    return bias.reshape(h, t, reach + 1, t).transpose(0, 2, 1, 3)


def _mixers(proj, fcol, frow, rel_table, lam_params, lam_init):
    b, s, _ = proj.shape
    t = min(ATTN_TILE, s)
    assert s % t == 0 and t % CHUNK == 0
    reach = -(-BAND_CHUNKS * CHUNK // t)
    o_a = _mixer_call(
        _fox_kernel, proj, 0, [fcol, frow],
        [pl.BlockSpec((1, t, LANES), lambda bi, qi: (bi, qi, 0)),
         pl.BlockSpec((1, 8, s), lambda bi, qi: (bi, 0, 0))], "fox")
    bias = _band_bias(rel_table, t, reach)
    o_b = _mixer_call(
        functools.partial(_band_kernel, reach=reach), proj, 1, [bias],
        [_resident(bias.shape)], "band")
    o_c = _mixer_call(
        functools.partial(_diff_kernel, lam_init=lam_init), proj, 2, [lam_params],
        [_resident(lam_params.shape)], "diff")
    o_d = _mixer_call(_stick_kernel, proj, 3, [], [], "stick")
    return o_a, o_b, o_c, o_d


def _outproj_kernel(x_ref, oa_ref, ob_ref, oc_ref, od_ref, w_ref, o_ref):
    acc = x_ref[...]
    for grp, m_ref in enumerate((oa_ref, ob_ref, oc_ref, od_ref)):
        acc = acc + jnp.dot(m_ref[...], w_ref[grp * GROUP_WIDTH:(grp + 1) * GROUP_WIDTH, :],
                            preferred_element_type=F32)
    o_ref[...] = acc


def _outproj(x2, mixed, w_out):
    n, d = x2.shape
    tm = min(TOKEN_TILE, n)
    gw = GROUP_WIDTH
    mix_spec = pl.BlockSpec((tm, gw), lambda i: (i, 0))
    return pl.pallas_call(
        _outproj_kernel,
        grid=(n // tm,),
        in_specs=[pl.BlockSpec((tm, d), lambda i: (i, 0)), mix_spec, mix_spec, mix_spec, mix_spec,
                  _resident(w_out.shape)],
        out_specs=pl.BlockSpec((tm, d), lambda i: (i, 0)),
        out_shape=jax.ShapeDtypeStruct((n, d), F32),
        compiler_params=_params(("arbitrary",)),
        name="outproj",
    )(x2, *[m.reshape(n, gw) for m in mixed], w_out.astype(BF16))


def _final_norm_kernel(x_ref, g_ref, o_ref):
    o_ref[...] = _rmsnorm(x_ref[...], g_ref[...])


def _final_norm(x2, g):
    n, d = x2.shape
    tm = min(TOKEN_TILE, n)
    return pl.pallas_call(
        _final_norm_kernel,
        grid=(n // tm,),
        in_specs=[pl.BlockSpec((tm, d), lambda i: (i, 0)), _resident((1, d))],
        out_specs=pl.BlockSpec((tm, d), lambda i: (i, 0)),
        out_shape=jax.ShapeDtypeStruct((n, d), F32),
        compiler_params=_params(("arbitrary",)),
        name="final_norm",
    )(x2, g.reshape(1, d))


def _lambda_init(layer_idx):
    return 0.8 - 0.6 * math.exp(-0.3 * layer_idx)


def kernel(x, g_ffn1, ffn1_w_gu, ffn1_w_down, g_mix, w_in, b_f, rel_bias, diff_lambda, w_out,
           g_ffn2, ffn2_w_gu, ffn2_w_down, g_final):
    b, s, d = x.shape
    x2 = x.reshape(b * s, d)
    for layer in range(g_ffn1.shape[0]):
        x2 = _ffn(x2, g_ffn1[layer], ffn1_w_gu[layer], ffn1_w_down[layer])
        proj, fcol, frow = _inproj(x2.reshape(b, s, d), g_mix[layer], w_in[layer], b_f[layer])
        mixed = _mixers(proj, fcol, frow, rel_bias[layer], diff_lambda[layer], _lambda_init(layer))
        x2 = _outproj(x2, mixed, w_out[layer])
        x2 = _ffn(x2, g_ffn2[layer], ffn2_w_gu[layer], ffn2_w_down[layer])
    return _final_norm(x2, g_final).reshape(b, s, d)
```

```python
import functools
import math

import numpy as np
import jax
import jax.numpy as jnp
from jax import lax
from jax.experimental import pallas as pl
from jax.experimental.pallas import tpu as pltpu

F32 = jnp.float32
BF16 = jnp.bfloat16

LANES = 128
SUBLANES = 8
HEAD_DIM = 64
HEADS_PER_GROUP = 4
GROUP_WIDTH = HEADS_PER_GROUP * HEAD_DIM
N_GROUPS = 4
CHUNK = 64
BAND_CHUNKS = 8
MAX_REL_DIST = 256
DIFF_QK_DIM = HEAD_DIM // 2
FFN_RES = 0.5
RMS_EPS = 1e-6
NEG = -1e30

TOKEN_TILE = 512
ATTN_TILE = 256
FFN_CHUNK = 1408
VMEM_LIMIT = 56 * 1024 * 1024

_NT = (((1,), (1,)), ((), ()))


def _rmsnorm(x, g):
    y = x * lax.rsqrt(jnp.mean(x * x, axis=-1, keepdims=True) + RMS_EPS)
    return y * g


def _log_sigmoid(x):
    return jnp.minimum(x, 0.0) - jnp.log1p(jnp.exp(-jnp.abs(x)))


def _split3(x):
    h1 = x.astype(BF16)
    r1 = x - h1.astype(F32)
    h2 = r1.astype(BF16)
    h3 = (r1 - h2.astype(F32)).astype(BF16)
    return h1, h2, h3


def _params(sem):
    return pltpu.CompilerParams(dimension_semantics=sem, vmem_limit_bytes=VMEM_LIMIT)


def _resident(shape):
    zeros = (0,) * len(shape)
    return pl.BlockSpec(shape, lambda *_: zeros, pipeline_mode=pl.Buffered(1))


def _ffn_kernel(x_ref, g_ref, wgu_ref, wd_ref, o_ref, *, d_ff, chunk):
    x = x_ref[...]
    h = _rmsnorm(x, g_ref[...]).astype(BF16)
    acc = jnp.zeros(x.shape, F32)
    for c0 in range(0, d_ff, chunk):
        gate = jnp.dot(h, wgu_ref[:, c0:c0 + chunk], preferred_element_type=F32)
        up = jnp.dot(h, wgu_ref[:, d_ff + c0:d_ff + c0 + chunk], preferred_element_type=F32)
        act = gate * jax.nn.sigmoid(gate) * up
        acc = acc + jnp.dot(act.astype(BF16), wd_ref[c0:c0 + chunk, :], preferred_element_type=F32)
    o_ref[...] = x + FFN_RES * acc


def _ffn(x2, g, w_gu, w_down):
    n, d = x2.shape
    d_ff = w_down.shape[0]
    chunk = FFN_CHUNK if d_ff % FFN_CHUNK == 0 else d_ff
    tm = min(TOKEN_TILE, n)
    return pl.pallas_call(
        functools.partial(_ffn_kernel, d_ff=d_ff, chunk=chunk),
        grid=(n // tm,),
        in_specs=[
            pl.BlockSpec((tm, d), lambda i: (i, 0)),
            _resident((1, d)),
            _resident((d, 2 * d_ff)),
            _resident((d_ff, d)),
        ],
        out_specs=pl.BlockSpec((tm, d), lambda i: (i, 0)),
        out_shape=jax.ShapeDtypeStruct((n, d), F32),
        compiler_params=_params(("arbitrary",)),
        name="ffn",
    )(x2, g.reshape(1, d), w_gu.astype(BF16), w_down.astype(BF16))


def _inproj_kernel(x_ref, g_ref, wqk_ref, cs_ref, wvt_ref, wf_ref, wft_ref, bcol_ref, brow_ref,
                   qk_ref, vt_ref, fkey_ref, frow_ref, ccol_ref, crow_ref, *, tm):
    @pl.when(pl.program_id(1) == 0)
    def _():
        ccol_ref[...] = jnp.zeros_like(ccol_ref)
        crow_ref[...] = jnp.zeros_like(crow_ref)

    h = _rmsnorm(x_ref[0], g_ref[...]).astype(BF16)
    qk = jnp.dot(h, wqk_ref[...], preferred_element_type=F32)
    qk_ref[0] = (qk * cs_ref[...]).astype(BF16)
    vt_ref[0] = lax.dot_general(wvt_ref[...], h, _NT, preferred_element_type=F32).astype(BF16)

    ri = lax.broadcasted_iota(jnp.int32, (tm, tm), 0)
    ci = lax.broadcasted_iota(jnp.int32, (tm, tm), 1)
    tril = jnp.where(ci <= ri, 1.0, 0.0).astype(BF16)

    ls_col = _log_sigmoid(jnp.dot(h, wf_ref[...], preferred_element_type=F32) + bcol_ref[...])
    cum_col = ccol_ref[...]
    for part in _split3(ls_col):
        cum_col = cum_col + jnp.dot(tril, part, preferred_element_type=F32)
    ccol_ref[...] = cum_col[tm - 1:tm, :]
    n_exp = fkey_ref.shape[-1]
    src = lax.broadcasted_iota(jnp.int32, (LANES, n_exp), 0)
    dst = lax.broadcasted_iota(jnp.int32, (LANES, n_exp), 1)
    expand = jnp.where(dst // LANES == src, 1.0, 0.0).astype(BF16)
    fkey = jnp.zeros((tm, n_exp), F32)
    for part in _split3(cum_col):
        fkey = fkey + jnp.dot(part, expand, preferred_element_type=F32)
    fkey_ref[0] = fkey

    ls_row = _log_sigmoid(
        lax.dot_general(wft_ref[...], h, _NT, preferred_element_type=F32) + brow_ref[...])
    cum_row = crow_ref[...]
    for part in _split3(ls_row):
        cum_row = cum_row + lax.dot_general(part, tril, _NT, preferred_element_type=F32)
    frow_ref[0] = cum_row
    crow_ref[...] = cum_row[:, tm - 1:tm]


def _inproj(x3, g, w_in, b_f):
    b, s, d = x3.shape
    gw = GROUP_WIDTH
    nf = HEADS_PER_GROUP
    starts = (0, 3 * gw + nf, 6 * gw + nf, 9 * gw + nf)
    w_qk = jnp.concatenate([w_in[:, st:st + 2 * gw] for st in starts], axis=1).astype(BF16)
    w_vt = jnp.concatenate([w_in[:, st + 2 * gw:st + 3 * gw] for st in starts], axis=1).T.astype(BF16)
    w_f = w_in[:, 3 * gw:3 * gw + nf]
    wf_col = jnp.pad(w_f, ((0, 0), (0, LANES - nf))).astype(BF16)
    wf_row = jnp.pad(w_f.T, ((0, SUBLANES - nf), (0, 0))).astype(BF16)
    b_col = jnp.pad(b_f, (0, LANES - nf)).reshape(1, LANES)
    b_row = jnp.pad(b_f, (0, SUBLANES - nf)).reshape(SUBLANES, 1)
    width = w_qk.shape[1]
    cs = np.ones((1, width), np.float32)
    for grp in range(N_GROUPS):
        scale = DIFF_QK_DIM ** -0.5 if grp == 2 else HEAD_DIM ** -0.5
        cs[0, 2 * grp * gw:(2 * grp + 1) * gw] = scale
    tm = min(TOKEN_TILE, s)
    n_exp = nf * LANES
    return pl.pallas_call(
        functools.partial(_inproj_kernel, tm=tm),
        grid=(b, s // tm),
        in_specs=[
            pl.BlockSpec((1, tm, d), lambda bi, j: (bi, j, 0)),
            _resident((1, d)),
            _resident((d, width)),
            _resident((1, width)),
            _resident((N_GROUPS * gw, d)),
            _resident((d, LANES)),
            _resident((SUBLANES, d)),
            _resident((1, LANES)),
            _resident((SUBLANES, 1)),
        ],
        out_specs=[
            pl.BlockSpec((1, tm, width), lambda bi, j: (bi, j, 0)),
            pl.BlockSpec((1, N_GROUPS * gw, tm), lambda bi, j: (bi, 0, j)),
            pl.BlockSpec((1, tm, n_exp), lambda bi, j: (bi, j, 0)),
            pl.BlockSpec((1, SUBLANES, tm), lambda bi, j: (bi, 0, j)),
        ],
        out_shape=[
            jax.ShapeDtypeStruct((b, s, width), BF16),
            jax.ShapeDtypeStruct((b, N_GROUPS * gw, s), BF16),
            jax.ShapeDtypeStruct((b, s, n_exp), F32),
            jax.ShapeDtypeStruct((b, SUBLANES, s), F32),
        ],
        scratch_shapes=[pltpu.VMEM((1, LANES), F32), pltpu.VMEM((SUBLANES, 1), F32)],
        compiler_params=_params(("arbitrary", "arbitrary")),
        name="inproj",
    )(x3, g.reshape(1, d), w_qk, jnp.asarray(cs), w_vt, wf_col, wf_row, b_col, b_row)


def _lane_range(lo, width):
    lane = lax.broadcasted_iota(jnp.int32, (1, LANES), 1)
    return (lane >= lo) & (lane < lo + width)


def _masked_queries(q_ref, width):
    out = []
    for pair in range(2):
        q = q_ref[0, :, pair * LANES:(pair + 1) * LANES]
        for lo in range(0, LANES, width):
            out.append(jnp.where(_lane_range(lo, width), q, jnp.zeros_like(q)))
    return out


def _head_scores(k_ref, qs, t, per_head):
    def scores(kb):
        k0 = pl.multiple_of(kb * t, t)
        out = []
        for c, q in enumerate(qs):
            pair = c // (2 * per_head)
            k = k_ref[0, pl.ds(k0, t), pair * LANES:(pair + 1) * LANES]
            out.append(lax.dot_general(k, q, _NT, preferred_element_type=F32))
        return out
    return scores


def _head_products(vt_ref, t, per_head, with_ones=True):
    def products(ps, kb):
        k0 = pl.multiple_of(kb * t, t)
        out = []
        for c, p in enumerate(ps):
            h = c // per_head
            vt = vt_ref[0, (h // 2) * LANES:(h // 2 + 1) * LANES, pl.ds(k0, t)]
            if with_ones:
                vt = _value_rows(vt, h % 2)
            out.append(jnp.dot(vt, p, preferred_element_type=F32))
        return out
    return products


def _value_rows(vt, hh):
    ones = jnp.ones((HEAD_DIM, vt.shape[1]), vt.dtype)
    if hh == 0:
        return jnp.concatenate([vt[:HEAD_DIM], ones], axis=0)
    return jnp.concatenate([ones, vt[HEAD_DIM:]], axis=0)


def _softmax_weights(s, m, shift=None):
    mx = jnp.max(s, axis=0, keepdims=True)
    if shift is not None:
        mx = mx + shift
    m_new = jnp.maximum(m, mx)
    alpha = jnp.exp(m - m_new)
    p = jnp.exp(s - (m_new if shift is None else m_new - shift))
    return m_new, alpha, p.astype(BF16)


def _flash_forward(kb_lo, kb_hi, t, n_chain, scores, weights, products, state0):
    chains = range(n_chain)

    def advance(s_cur, kb, diag, p_prev, alpha_prev, state, acc, kb_prev):
        prod = products(p_prev, kb_prev)
        new = [weights(c, s_cur[c], kb, diag, state[c]) for c in chains]
        acc = tuple(alpha_prev[c] * acc[c] + prod[c] for c in chains)
        return (tuple(n[2] for n in new), tuple(n[1] for n in new), tuple(n[0] for n in new), acc)

    def body(kb, carry):
        s_cur, p_prev, alpha_prev, state, acc = carry
        s_next = tuple(scores(kb + 1))
        return (s_next,) + advance(s_cur, kb, False, p_prev, alpha_prev, state, acc,
                                   jnp.maximum(kb - 1, kb_lo))

    init = (tuple(scores(kb_lo)),
            tuple(jnp.zeros((t, t), BF16) for _ in chains),
            tuple(jnp.ones((1, t), F32) for _ in chains),
            tuple(state0),
            tuple(jnp.zeros((LANES, t), F32) for _ in chains))
    s_cur, p_prev, alpha_prev, state, acc = lax.fori_loop(kb_lo, kb_hi, body, init)
    p_last, alpha, _, acc = advance(s_cur, kb_hi, True, p_prev, alpha_prev, state, acc,
                                    jnp.maximum(kb_hi - 1, kb_lo))
    prod = products(p_last, kb_hi)
    return [alpha[c] * acc[c] + prod[c] for c in chains]


def _normalized(acc, hh):
    if hh == 0:
        return acc[:HEAD_DIM] * (1.0 / acc[HEAD_DIM:HEAD_DIM + 1])
    return acc[HEAD_DIM:] * (1.0 / acc[0:1])


def _store_pair(o_ref, pair, top, bottom):
    o_t = jnp.concatenate([top, bottom], axis=0)
    o_ref[0, :, pair * LANES:(pair + 1) * LANES] = o_t.T.astype(o_ref.dtype)


def _tile_iotas(t):
    key = lax.broadcasted_iota(jnp.int32, (t, t), 0)
    qry = lax.broadcasted_iota(jnp.int32, (t, t), 1)
    return key, qry


def _fox_kernel(q_ref, k_ref, vt_ref, fkey_ref, frow_ref, o_ref, *, t):
    qi = pl.program_id(1)
    q0 = pl.multiple_of(qi * t, t)
    key, qry = _tile_iotas(t)
    causal = key <= qry
    qs = _masked_queries(q_ref, HEAD_DIM)
    f_q = [frow_ref[0, h:h + 1, pl.ds(q0, t)] for h in range(HEADS_PER_GROUP)]

    def weights(h, s, kb, diag, m):
        k0 = pl.multiple_of(kb * t, t)
        f_k = fkey_ref[0, pl.ds(k0, t), h * LANES:(h + 1) * LANES]
        s = s - jnp.concatenate([f_k] * (t // LANES), axis=1)
        if diag:
            s = jnp.where(causal, s, NEG)
        return _softmax_weights(s, m, shift=f_q[h])

    acc = _flash_forward(0, qi, t, HEADS_PER_GROUP, _head_scores(k_ref, qs, t, 1), weights,
                         _head_products(vt_ref, t, 1),
                         [jnp.full((1, t), NEG, F32)] * HEADS_PER_GROUP)
    for pair in range(2):
        _store_pair(o_ref, pair, _normalized(acc[2 * pair], 0), _normalized(acc[2 * pair + 1], 1))


def _band_kernel(q_ref, k_ref, vt_ref, bias_ref, o_ref, *, t, reach):
    qi = pl.program_id(1)
    qs = _masked_queries(q_ref, HEAD_DIM)

    def weights(h, s, kb, diag, m):
        bias = bias_ref[h, pl.ds(kb - qi + reach, 1)][0]
        return _softmax_weights(s + bias, m)

    acc = _flash_forward(jnp.maximum(qi - reach, 0), qi, t, HEADS_PER_GROUP,
                         _head_scores(k_ref, qs, t, 1), weights, _head_products(vt_ref, t, 1),
                         [jnp.full((1, t), NEG, F32)] * HEADS_PER_GROUP)
    for pair in range(2):
        _store_pair(o_ref, pair, _normalized(acc[2 * pair], 0), _normalized(acc[2 * pair + 1], 1))


def _diff_kernel(q_ref, k_ref, vt_ref, lam_ref, o_ref, *, t, lam_init):
    qi = pl.program_id(1)
    key, qry = _tile_iotas(t)
    visible = (key // CHUNK) <= (qry // CHUNK)
    dist = (qry - key).astype(F32)
    lp = lam_ref[...]
    lam = (jnp.exp(jnp.sum(lp[0:1] * lp[1:2], axis=-1, keepdims=True))
           - jnp.exp(jnp.sum(lp[2:3] * lp[3:4], axis=-1, keepdims=True)) + lam_init)
    slopes = [2.0 ** (-8.0 * (h + 1) / HEADS_PER_GROUP) for h in range(HEADS_PER_GROUP)]
    rel_off = [dist * (-sl) for sl in slopes]
    rel_diag = [jnp.where(visible, jnp.abs(dist) * (-sl), NEG) for sl in slopes]
    qs = _masked_queries(q_ref, DIFF_QK_DIM)

    def weights(c, s, kb, diag, m):
        h = c // 2
        if diag:
            return _softmax_weights(s + rel_diag[h], m)
        shift = (-slopes[h] * t) * (qi - kb).astype(F32)
        return _softmax_weights(s + rel_off[h], m, shift=shift)

    n_chain = 2 * HEADS_PER_GROUP
    acc = _flash_forward(0, qi, t, n_chain, _head_scores(k_ref, qs, t, 2), weights,
                         _head_products(vt_ref, t, 2), [jnp.full((1, t), NEG, F32)] * n_chain)
    for pair in range(2):
        halves = []
        for hh in range(2):
            h = 2 * pair + hh
            o_h = _normalized(acc[2 * h], hh) - lam * _normalized(acc[2 * h + 1], hh)
            ms = jnp.mean(o_h * o_h, axis=0, keepdims=True)
            halves.append(o_h * lax.rsqrt(ms + RMS_EPS) * (1.0 - lam_init))
        _store_pair(o_ref, pair, halves[0], halves[1])


def _stick_kernel(q_ref, k_ref, vt_ref, o_ref, *, t):
    qi = pl.program_id(1)
    key, qry = _tile_iotas(t)
    strict = key < qry
    suffix_ones = jnp.where(qry >= key, 1.0, 0.0).astype(BF16)
    qs = _masked_queries(q_ref, HEAD_DIM)
    chains = range(HEADS_PER_GROUP)
    scores = _head_scores(k_ref, qs, t, 1)
    products = _head_products(vt_ref, t, 1, with_ones=False)

    def weights(zs, tails, diag):
        rest = []
        for z in zs:
            log_rest = _log_sigmoid(-z)
            rest.append(jnp.where(strict, log_rest, 0.0) if diag else log_rest)
        his = [r.astype(BF16) for r in rest]
        los = [(r - hi.astype(F32)).astype(BF16) for r, hi in zip(rest, his)]
        within = [jnp.dot(suffix_ones, hi, preferred_element_type=F32)
                  + jnp.dot(suffix_ones, lo, preferred_element_type=F32)
                  for hi, lo in zip(his, los)]
        ws = []
        for z, wi, tail in zip(zs, within, tails):
            w = jnp.exp(z + wi + tail)
            ws.append((jnp.where(strict, w, 0.0) if diag else w).astype(BF16))
        return tuple(ws), tuple(tail + wi[0:1, :] for tail, wi in zip(tails, within))

    def body(it, carry):
        z_cur, w_prev, tails, acc = carry
        kb = qi - 1 - it
        z_next = tuple(scores(jnp.maximum(kb - 1, 0)))
        prod = products(w_prev, kb + 1)
        ws, tails = weights(z_cur, tails, False)
        return z_next, ws, tails, tuple(a + p for a, p in zip(acc, prod))

    z_diag = scores(qi)
    z_next = tuple(scores(jnp.maximum(qi - 1, 0)))
    ws, tails = weights(z_diag, [jnp.zeros((1, t), F32)] * HEADS_PER_GROUP, True)
    acc0 = tuple(jnp.zeros((LANES, t), F32) for _ in chains)
    _, w_last, _, acc = lax.fori_loop(0, qi, body, (z_next, ws, tails, acc0))
    prod = products(w_last, 0)
    acc = [a + p for a, p in zip(acc, prod)]
    for pair in range(2):
        _store_pair(o_ref, pair, acc[2 * pair][:HEAD_DIM], acc[2 * pair + 1][HEAD_DIM:])


def _mixer_call(body, qk, vt, group, extra_inputs, extra_specs, name):
    b, s, _ = qk.shape
    t = min(ATTN_TILE, s)
    gw = GROUP_WIDTH
    return pl.pallas_call(
        functools.partial(body, t=t),
        grid=(b, s // t),
        in_specs=[
            pl.BlockSpec((1, t, gw), lambda bi, qi: (bi, qi, 2 * group)),
            pl.BlockSpec((1, s, gw), lambda bi, qi: (bi, 0, 2 * group + 1)),
            pl.BlockSpec((1, gw, s), lambda bi, qi: (bi, group, 0)),
        ] + extra_specs,
        out_specs=pl.BlockSpec((1, t, gw), lambda bi, qi: (bi, qi, 0)),
        out_shape=jax.ShapeDtypeStruct((b, s, gw), BF16),
        compiler_params=_params(("arbitrary", "arbitrary")),
        name=name,
    )(qk, qk, vt, *extra_inputs)


def _band_bias(rel_table, t, reach):
    n_keys = (reach + 1) * t
    period = n_keys + t
    d = np.arange(period)
    d = np.where(d < t, d, d - period)
    idx = np.clip(reach * t + d, -MAX_REL_DIST, MAX_REL_DIST) + MAX_REL_DIST
    ext = rel_table.astype(F32)[:, idx]
    h = rel_table.shape[0]
    rows = jnp.tile(ext, (1, n_keys))[:, :n_keys * (period - 1)].reshape(h, n_keys, period - 1)
    bias = rows[:, :, :t]
    kc = (np.arange(n_keys) // CHUNK - (reach * t // CHUNK - BAND_CHUNKS))[:, None]
    qc = (np.arange(t) // CHUNK)[None, :]
    visible = (kc >= qc) & (kc <= qc + BAND_CHUNKS)
    return jnp.where(visible[None], bias, NEG).reshape(h, reach + 1, t, t)


def _mixers(qk, vt, fkey, frow, rel_table, lam_params, lam_init):
    b, s, _ = qk.shape
    t = min(ATTN_TILE, s)
    assert s % t == 0 and t % CHUNK == 0 and t % LANES == 0
    reach = -(-BAND_CHUNKS * CHUNK // t)
    o_a = _mixer_call(
        _fox_kernel, qk, vt, 0, [fkey, frow],
        [pl.BlockSpec((1, s, fkey.shape[-1]), lambda bi, qi: (bi, 0, 0)),
         pl.BlockSpec((1, SUBLANES, s), lambda bi, qi: (bi, 0, 0))], "fox")
    bias = _band_bias(rel_table, t, reach)
    o_b = _mixer_call(
        functools.partial(_band_kernel, reach=reach), qk, vt, 1, [bias],
        [_resident(bias.shape)], "band")
    o_c = _mixer_call(
        functools.partial(_diff_kernel, lam_init=lam_init), qk, vt, 2, [lam_params],
        [_resident(lam_params.shape)], "diff")
    o_d = _mixer_call(_stick_kernel, qk, vt, 3, [], [], "stick")
    return o_a, o_b, o_c, o_d


def _outproj_kernel(x_ref, oa_ref, ob_ref, oc_ref, od_ref, w_ref, o_ref):
    acc = x_ref[...]
    for grp, m_ref in enumerate((oa_ref, ob_ref, oc_ref, od_ref)):
        acc = acc + jnp.dot(m_ref[...], w_ref[grp * GROUP_WIDTH:(grp + 1) * GROUP_WIDTH, :],
                            preferred_element_type=F32)
    o_ref[...] = acc


def _outproj(x2, mixed, w_out):
    n, d = x2.shape
    tm = min(TOKEN_TILE, n)
    gw = GROUP_WIDTH
    mix_spec = pl.BlockSpec((tm, gw), lambda i: (i, 0))
    return pl.pallas_call(
        _outproj_kernel,
        grid=(n // tm,),
        in_specs=[pl.BlockSpec((tm, d), lambda i: (i, 0)), mix_spec, mix_spec, mix_spec, mix_spec,
                  _resident(w_out.shape)],
        out_specs=pl.BlockSpec((tm, d), lambda i: (i, 0)),
        out_shape=jax.ShapeDtypeStruct((n, d), F32),
        compiler_params=_params(("arbitrary",)),
        name="outproj",
    )(x2, *[m.reshape(n, gw) for m in mixed], w_out.astype(BF16))


def _final_norm_kernel(x_ref, g_ref, o_ref):
    o_ref[...] = _rmsnorm(x_ref[...], g_ref[...])


def _final_norm(x2, g):
    n, d = x2.shape
    tm = min(TOKEN_TILE, n)
    return pl.pallas_call(
        _final_norm_kernel,
        grid=(n // tm,),
        in_specs=[pl.BlockSpec((tm, d), lambda i: (i, 0)), _resident((1, d))],
        out_specs=pl.BlockSpec((tm, d), lambda i: (i, 0)),
        out_shape=jax.ShapeDtypeStruct((n, d), F32),
        compiler_params=_params(("arbitrary",)),
        name="final_norm",
    )(x2, g.reshape(1, d))


def _lambda_init(layer_idx):
    return 0.8 - 0.6 * math.exp(-0.3 * layer_idx)


def kernel(x, g_ffn1, ffn1_w_gu, ffn1_w_down, g_mix, w_in, b_f, rel_bias, diff_lambda, w_out,
           g_ffn2, ffn2_w_gu, ffn2_w_down, g_final):
    b, s, d = x.shape
    x2 = x.reshape(b * s, d)
    for layer in range(g_ffn1.shape[0]):
        x2 = _ffn(x2, g_ffn1[layer], ffn1_w_gu[layer], ffn1_w_down[layer])
        qk, vt, fkey, frow = _inproj(x2.reshape(b, s, d), g_mix[layer], w_in[layer], b_f[layer])
        mixed = _mixers(qk, vt, fkey, frow, rel_bias[layer], diff_lambda[layer], _lambda_init(layer))
        x2 = _outproj(x2, mixed, w_out[layer])
        x2 = _ffn(x2, g_ffn2[layer], ffn2_w_gu[layer], ffn2_w_down[layer])
    return _final_norm(x2, g_final).reshape(b, s, d)
```

```python
import functools
import math

import numpy as np
import jax
import jax.numpy as jnp
from jax import lax
from jax.experimental import pallas as pl
from jax.experimental.pallas import tpu as pltpu

F32 = jnp.float32
BF16 = jnp.bfloat16

LANES = 128
HEAD_DIM = 64
HEADS_PER_GROUP = 4
GROUP_WIDTH = HEADS_PER_GROUP * HEAD_DIM
N_GROUPS = 4
CHUNK = 64
BAND_CHUNKS = 8
MAX_REL_DIST = 256
DIFF_QK_DIM = HEAD_DIM // 2
FFN_RES = 0.5
RMS_EPS = 1e-6
NEG = -1e30
LOG2E = math.log2(math.e)
N_SPLIT = 3

TOKEN_TILE = 512
ATTN_TILE = 256
FFN_CHUNK = 1408
VMEM_LIMIT = 56 * 1024 * 1024

_NT = (((1,), (1,)), ((), ()))


def _rmsnorm(x, g):
    y = x * lax.rsqrt(jnp.mean(x * x, axis=-1, keepdims=True) + RMS_EPS)
    return y * g


def _log_sigmoid(x):
    return jnp.minimum(x, 0.0) - jnp.log1p(jnp.exp(-jnp.abs(x)))


def _split3(x):
    h1 = x.astype(BF16)
    r1 = x - h1.astype(F32)
    h2 = r1.astype(BF16)
    h3 = (r1 - h2.astype(F32)).astype(BF16)
    return h1, h2, h3


def _split3_const(c):
    parts = []
    for _ in range(N_SPLIT):
        p = float(np.asarray(c, np.float32).astype(jnp.bfloat16).astype(np.float32))
        parts.append(p)
        c = c - p
    return parts


def _params(sem):
    return pltpu.CompilerParams(dimension_semantics=sem, vmem_limit_bytes=VMEM_LIMIT)


def _resident(shape):
    zeros = (0,) * len(shape)
    return pl.BlockSpec(shape, lambda *_: zeros, pipeline_mode=pl.Buffered(1))


def _ffn_kernel(x_ref, g_ref, wgu_ref, wd_ref, o_ref, *, d_ff, chunk):
    x = x_ref[...]
    h = _rmsnorm(x, g_ref[...]).astype(BF16)
    acc = jnp.zeros(x.shape, F32)
    for c0 in range(0, d_ff, chunk):
        gate = jnp.dot(h, wgu_ref[:, c0:c0 + chunk], preferred_element_type=F32)
        up = jnp.dot(h, wgu_ref[:, d_ff + c0:d_ff + c0 + chunk], preferred_element_type=F32)
        act = gate * jax.nn.sigmoid(gate) * up
        acc = acc + jnp.dot(act.astype(BF16), wd_ref[c0:c0 + chunk, :], preferred_element_type=F32)
    o_ref[...] = x + FFN_RES * acc


def _ffn(x2, g, w_gu, w_down):
    n, d = x2.shape
    d_ff = w_down.shape[0]
    chunk = FFN_CHUNK if d_ff % FFN_CHUNK == 0 else d_ff
    tm = min(TOKEN_TILE, n)
    return pl.pallas_call(
        functools.partial(_ffn_kernel, d_ff=d_ff, chunk=chunk),
        grid=(n // tm,),
        in_specs=[
            pl.BlockSpec((tm, d), lambda i: (i, 0)),
            _resident((1, d)),
            _resident((d, 2 * d_ff)),
            _resident((d_ff, d)),
        ],
        out_specs=pl.BlockSpec((tm, d), lambda i: (i, 0)),
        out_shape=jax.ShapeDtypeStruct((n, d), F32),
        compiler_params=_params(("arbitrary",)),
        name="ffn",
    )(x2, g.reshape(1, d), w_gu.astype(BF16), w_down.astype(BF16))


def _inproj_kernel(x_ref, g_ref, wqk_ref, cs_ref, wvt_ref, wf_ref, bf_ref, place_ref, konst_ref,
                   qk_ref, vt_ref, kaug_ref, qaug_ref, carry_ref, *, tm):
    @pl.when(pl.program_id(1) == 0)
    def _():
        carry_ref[...] = jnp.zeros_like(carry_ref)

    h = _rmsnorm(x_ref[0], g_ref[...]).astype(BF16)
    qk = jnp.dot(h, wqk_ref[...], preferred_element_type=F32)
    qk_ref[0] = (qk * cs_ref[...]).astype(BF16)
    vt_ref[0] = lax.dot_general(wvt_ref[...], h, _NT, preferred_element_type=F32).astype(BF16)

    ri = lax.broadcasted_iota(jnp.int32, (tm, tm), 0)
    ci = lax.broadcasted_iota(jnp.int32, (tm, tm), 1)
    tril = jnp.where(ci <= ri, 1.0, 0.0).astype(BF16)
    ls = _log_sigmoid(jnp.dot(h, wf_ref[...], preferred_element_type=F32) + bf_ref[...])
    cum = carry_ref[...]
    for part in _split3(ls):
        cum = cum + jnp.dot(tril, part, preferred_element_type=F32)
    carry_ref[...] = cum[tm - 1:tm, :]

    kaug = konst_ref[0:1, :]
    qaug = konst_ref[1:2, :]
    for i, part in enumerate(_split3(cum * LOG2E)):
        kaug = kaug + jnp.dot(part, place_ref[i], preferred_element_type=F32)
        qaug = qaug + jnp.dot(part, place_ref[N_SPLIT + i], preferred_element_type=F32)
    kaug_ref[0] = kaug.astype(BF16)
    qaug_ref[0] = qaug.astype(BF16)


def _inproj(x3, g, w_in, b_f):
    b, s, d = x3.shape
    gw = GROUP_WIDTH
    nf = HEADS_PER_GROUP
    starts = (0, 3 * gw + nf, 6 * gw + nf, 9 * gw + nf)
    w_qk = jnp.concatenate([w_in[:, st:st + 2 * gw] for st in starts], axis=1).astype(BF16)
    w_vt = jnp.concatenate([w_in[:, st + 2 * gw:st + 3 * gw] for st in starts], axis=1).T.astype(BF16)
    w_f = jnp.pad(w_in[:, 3 * gw:3 * gw + nf], ((0, 0), (0, LANES - nf))).astype(BF16)
    b_col = jnp.pad(b_f, (0, LANES - nf)).reshape(1, LANES)
    width = w_qk.shape[1]
    cs = np.ones((1, width), np.float32)
    for grp in range(N_GROUPS):
        scale = DIFF_QK_DIM ** -0.5 if grp == 2 else HEAD_DIM ** -0.5
        cs[0, 2 * grp * gw:(2 * grp + 1) * gw] = scale * LOG2E
    n_aug = nf * LANES
    place = np.zeros((2 * N_SPLIT, LANES, n_aug), np.float32)
    konst = np.zeros((2, n_aug), np.float32)
    for hd in range(nf):
        for i in range(N_SPLIT):
            place[i, hd, hd * LANES + i] = 1.0
            place[N_SPLIT + i, hd, hd * LANES + N_SPLIT + i] = 1.0
            konst[0, hd * LANES + N_SPLIT + i] = 1.0
            konst[1, hd * LANES + i] = -1.0
    tm = min(TOKEN_TILE, s)
    return pl.pallas_call(
        functools.partial(_inproj_kernel, tm=tm),
        grid=(b, s // tm),
        in_specs=[
            pl.BlockSpec((1, tm, d), lambda bi, j: (bi, j, 0)),
            _resident((1, d)),
            _resident((d, width)),
            _resident((1, width)),
            _resident((N_GROUPS * gw, d)),
            _resident((d, LANES)),
            _resident((1, LANES)),
            _resident(place.shape),
            _resident(konst.shape),
        ],
        out_specs=[
            pl.BlockSpec((1, tm, width), lambda bi, j: (bi, j, 0)),
            pl.BlockSpec((1, N_GROUPS * gw, tm), lambda bi, j: (bi, 0, j)),
            pl.BlockSpec((1, tm, n_aug), lambda bi, j: (bi, j, 0)),
            pl.BlockSpec((1, tm, n_aug), lambda bi, j: (bi, j, 0)),
        ],
        out_shape=[
            jax.ShapeDtypeStruct((b, s, width), BF16),
            jax.ShapeDtypeStruct((b, N_GROUPS * gw, s), BF16),
            jax.ShapeDtypeStruct((b, s, n_aug), BF16),
            jax.ShapeDtypeStruct((b, s, n_aug), BF16),
        ],
        scratch_shapes=[pltpu.VMEM((1, LANES), F32)],
        compiler_params=_params(("arbitrary", "arbitrary")),
        name="inproj",
    )(x3, g.reshape(1, d), w_qk, jnp.asarray(cs), w_vt, w_f, b_col,
      jnp.asarray(place, BF16), jnp.asarray(konst))


def _lane_range(lo, width):
    lane = lax.broadcasted_iota(jnp.int32, (1, LANES), 1)
    return (lane >= lo) & (lane < lo + width)


def _masked_queries(q_ref, width):
    out = []
    for pair in range(2):
        q = q_ref[0, :, pair * LANES:(pair + 1) * LANES]
        for lo in range(0, LANES, width):
            out.append(jnp.where(_lane_range(lo, width), q, jnp.zeros_like(q)))
    return out


def _head_scores(k_ref, qs, t, per_head, k_extra=None):
    def scores(kb):
        k0 = pl.multiple_of(kb * t, t)
        out = []
        for c, q in enumerate(qs):
            pair = c // (2 * per_head)
            k = k_ref[0, pl.ds(k0, t), pair * LANES:(pair + 1) * LANES]
            if k_extra is not None:
                k = jnp.concatenate([k, k_extra(c, k0)], axis=1)
            out.append(lax.dot_general(k, q, _NT, preferred_element_type=F32))
        return out
    return scores


def _value_rows(vt, hh):
    ones = jnp.ones((HEAD_DIM, vt.shape[1]), vt.dtype)
    if hh == 0:
        return jnp.concatenate([vt[:HEAD_DIM], ones], axis=0)
    return jnp.concatenate([ones, vt[HEAD_DIM:]], axis=0)


def _head_products(vt_ref, t, per_head, with_ones=True):
    def products(ps, kb):
        k0 = pl.multiple_of(kb * t, t)
        out = []
        for c, p in enumerate(ps):
            h = c // per_head
            vt = vt_ref[0, (h // 2) * LANES:(h // 2 + 1) * LANES, pl.ds(k0, t)]
            if with_ones:
                vt = _value_rows(vt, h % 2)
            out.append(jnp.dot(vt, p, preferred_element_type=F32))
        return out
    return products


_MAX, _ALPHA, _OFFSET = range(3)


def _flash_forward(kb_lo, kb_hi, t, n_chain, scores, adjust, shift_of, products,
                   s_scr, p_scr, acc_scr, st_scr):
    chains = range(n_chain)

    def land(kb, tiles, diag):
        slot = kb & 1
        for c in chains:
            s = adjust(c, tiles[c], kb, diag)
            s_scr[slot, c] = s
            mx = jnp.max(s, axis=0, keepdims=True)
            shift = shift_of(c, kb, diag)
            if shift is not None:
                mx = mx + shift
            m_old = st_scr[_MAX, c]
            m_new = jnp.maximum(m_old, mx)
            st_scr[_MAX, c] = m_new
            st_scr[_ALPHA, c] = jnp.exp2(m_old - m_new)
            st_scr[_OFFSET, c] = m_new if shift is None else m_new - shift

    def weigh(kb):
        slot = kb & 1
        for c in chains:
            r = st_scr[_OFFSET, c]
            for lo in range(0, t, LANES):
                cols = slice(lo, lo + LANES)
                p_scr[c, :, cols] = jnp.exp2(s_scr[slot, c, :, cols] - r[:, cols]).astype(BF16)
        return products([p_scr[c] for c in chains], kb)

    def accumulate(alpha, prod):
        for c in chains:
            acc_scr[c] = alpha[c] * acc_scr[c] + prod[c]

    def step(kb, diag_next):
        alpha = [st_scr[_ALPHA, c] for c in chains]
        nxt = scores(kb + 1)
        prod = weigh(kb)
        land(kb + 1, nxt, diag_next)
        accumulate(alpha, prod)

    for c in chains:
        st_scr[_MAX, c] = jnp.full((1, t), NEG, F32)
        acc_scr[c] = jnp.zeros((LANES, t), F32)

    @pl.when(kb_lo == kb_hi)
    def _():
        land(kb_lo, scores(kb_lo), True)

    @pl.when(kb_lo != kb_hi)
    def _():
        land(kb_lo, scores(kb_lo), False)

    def body(kb, carry):
        step(kb, False)
        return carry

    lax.fori_loop(kb_lo, kb_hi - 1, body, 0)

    @pl.when(kb_hi > kb_lo)
    def _():
        step(kb_hi - 1, True)

    alpha = [st_scr[_ALPHA, c] for c in chains]
    prod = weigh(kb_hi)
    return [alpha[c] * acc_scr[c] + prod[c] for c in chains]


def _normalized(acc, hh):
    if hh == 0:
        return acc[:HEAD_DIM] * (1.0 / acc[HEAD_DIM:HEAD_DIM + 1])
    return acc[HEAD_DIM:] * (1.0 / acc[0:1])


def _store_pair(o_ref, pair, top, bottom):
    o_t = jnp.concatenate([top, bottom], axis=0)
    o_ref[0, :, pair * LANES:(pair + 1) * LANES] = o_t.T.astype(o_ref.dtype)


def _store_normalized(o_ref, acc):
    for pair in range(2):
        _store_pair(o_ref, pair, _normalized(acc[2 * pair], 0), _normalized(acc[2 * pair + 1], 1))


def _tile_iotas(t):
    key = lax.broadcasted_iota(jnp.int32, (t, t), 0)
    qry = lax.broadcasted_iota(jnp.int32, (t, t), 1)
    return key, qry


def _no_shift(c, kb, diag):
    return None


def _fox_kernel(q_ref, k_ref, vt_ref, qaug_ref, kaug_ref, o_ref, s_scr, p_scr, acc_scr, st_scr,
                *, t):
    qi = pl.program_id(1)
    qs = [jnp.concatenate([q, qaug_ref[0, :, h * LANES:(h + 1) * LANES]], axis=1)
          for h, q in enumerate(_masked_queries(q_ref, HEAD_DIM))]

    def k_extra(h, k0):
        return kaug_ref[0, pl.ds(k0, t), h * LANES:(h + 1) * LANES]

    def adjust(h, s, kb, diag):
        if not diag:
            return s
        key, qry = _tile_iotas(t)
        return jnp.where(key <= qry, s, NEG)

    acc = _flash_forward(0, qi, t, HEADS_PER_GROUP, _head_scores(k_ref, qs, t, 1, k_extra),
                         adjust, _no_shift, _head_products(vt_ref, t, 1),
                         s_scr, p_scr, acc_scr, st_scr)
    _store_normalized(o_ref, acc)


def _band_kernel(q_ref, k_ref, vt_ref, bias_ref, o_ref, s_scr, p_scr, acc_scr, st_scr,
                 *, t, reach):
    qi = pl.program_id(1)
    qs = _masked_queries(q_ref, HEAD_DIM)

    def adjust(h, s, kb, diag):
        return s + bias_ref[h, kb - qi + reach]

    acc = _flash_forward(jnp.maximum(qi - reach, 0), qi, t, HEADS_PER_GROUP,
                         _head_scores(k_ref, qs, t, 1), adjust, _no_shift,
                         _head_products(vt_ref, t, 1), s_scr, p_scr, acc_scr, st_scr)
    _store_normalized(o_ref, acc)


def _alibi_lanes(t):
    q_aug = np.zeros((HEADS_PER_GROUP, t, LANES), np.float32)
    k_aug = np.zeros((HEADS_PER_GROUP, t, LANES), np.float32)
    off = np.arange(t, dtype=np.float32)
    for h in range(HEADS_PER_GROUP):
        parts = _split3_const(_alibi_slope(h) * LOG2E)
        for i, c in enumerate(parts):
            q_aug[h, :, i] = off
            k_aug[h, :, i] = -c
            q_aug[h, :, N_SPLIT + i] = c
            k_aug[h, :, N_SPLIT + i] = off
    return jnp.asarray(q_aug, BF16), jnp.asarray(k_aug, BF16)


def _alibi_slope(h):
    return 2.0 ** (-8.0 * (h + 1) / HEADS_PER_GROUP)


def _alibi_diag_fix(t):
    key = np.arange(t)[:, None]
    qry = np.arange(t)[None, :]
    dist = np.minimum(qry - key, 0).astype(np.float32)
    fix = np.stack([dist * np.float32(2.0 * _alibi_slope(h) * LOG2E) for h in range(HEADS_PER_GROUP)])
    return jnp.asarray(np.where((key // CHUNK <= qry // CHUNK)[None], fix, np.float32(NEG)))


def _diff_kernel(q_ref, k_ref, vt_ref, lam_ref, qal_ref, kal_ref, fix_ref, o_ref,
                 s_scr, p_scr, acc_scr, st_scr, *, t, lam_init):
    qi = pl.program_id(1)
    lp = lam_ref[...]
    lam = (jnp.exp(jnp.sum(lp[0:1] * lp[1:2], axis=-1, keepdims=True))
           - jnp.exp(jnp.sum(lp[2:3] * lp[3:4], axis=-1, keepdims=True)) + lam_init)
    qs = [jnp.concatenate([q, qal_ref[c // 2]], axis=1)
          for c, q in enumerate(_masked_queries(q_ref, DIFF_QK_DIM))]

    def k_extra(c, k0):
        return kal_ref[c // 2]

    def adjust(c, s, kb, diag):
        return s + fix_ref[c // 2] if diag else s

    def shift_of(c, kb, diag):
        if diag:
            return None
        return (-_alibi_slope(c // 2) * LOG2E * t) * (qi - kb).astype(F32)

    n_chain = 2 * HEADS_PER_GROUP
    acc = _flash_forward(0, qi, t, n_chain, _head_scores(k_ref, qs, t, 2, k_extra), adjust,
                         shift_of, _head_products(vt_ref, t, 2), s_scr, p_scr, acc_scr, st_scr)
    for pair in range(2):
        halves = []
        for hh in range(2):
            h = 2 * pair + hh
            o_h = _normalized(acc[2 * h], hh) - lam * _normalized(acc[2 * h + 1], hh)
            ms = jnp.mean(o_h * o_h, axis=0, keepdims=True)
            halves.append(o_h * lax.rsqrt(ms + RMS_EPS) * (1.0 - lam_init))
        _store_pair(o_ref, pair, halves[0], halves[1])


def _softplus2(z):
    return jnp.maximum(z, 0.0) + jnp.log(1.0 + jnp.exp2(-jnp.abs(z))) * LOG2E


def _stick_kernel(q_ref, k_ref, vt_ref, o_ref, z_scr, w_scr, acc_scr, st_scr, wi_scr, *, t):
    qi = pl.program_id(1)
    qs = _masked_queries(q_ref, HEAD_DIM)
    chains = range(HEADS_PER_GROUP)
    scores = _head_scores(k_ref, qs, t, 1)
    products = _head_products(vt_ref, t, 1, with_ones=False)
    tail_scr = st_scr.at[0]

    def land(j, tiles, diag):
        slot = j & 1
        key, qry = _tile_iotas(t)
        suffix_ones = jnp.where(qry >= key, 1.0, 0.0).astype(BF16)
        within = []
        for c in chains:
            z = tiles[c]
            rest = -_softplus2(z)
            if diag:
                rest = jnp.where(key < qry, rest, 0.0)
                z = jnp.where(key < qry, z, NEG)
            z_scr[slot, c] = z
            hi = rest.astype(BF16)
            lo = (rest - hi.astype(F32)).astype(BF16)
            within.append(jnp.dot(suffix_ones, hi, preferred_element_type=F32)
                          + jnp.dot(suffix_ones, lo, preferred_element_type=F32))
        return within

    def weigh(j):
        slot = j & 1
        for c in chains:
            tail = tail_scr[c]
            for lo in range(0, t, LANES):
                cols = slice(lo, lo + LANES)
                logw = z_scr[slot, c, :, cols] + wi_scr[c, :, cols] + tail[:, cols]
                w_scr[c, :, cols] = jnp.exp2(logw).astype(BF16)
            tail_scr[c] = tail + wi_scr[c, 0:1, :]
        return products([w_scr[c] for c in chains], qi - j)

    def collect(prod, within):
        for c in chains:
            acc_scr[c] = acc_scr[c] + prod[c]
            wi_scr[c] = within[c]

    def body(j, carry):
        nxt = scores(qi - j - 1)
        prod = weigh(j)
        collect(prod, land(j + 1, nxt, False))
        return carry

    for c, wi in enumerate(land(0, scores(qi), True)):
        tail_scr[c] = jnp.zeros((1, t), F32)
        acc_scr[c] = jnp.zeros((LANES, t), F32)
        wi_scr[c] = wi
    lax.fori_loop(0, qi, body, 0)
    prod = weigh(qi)
    acc = [acc_scr[c] + prod[c] for c in chains]
    for pair in range(2):
        _store_pair(o_ref, pair, acc[2 * pair][:HEAD_DIM], acc[2 * pair + 1][HEAD_DIM:])


def _mixer_call(body, qk, vt, group, n_chain, extra_inputs, extra_specs, name, extra_scratch=()):
    b, s, _ = qk.shape
    t = min(ATTN_TILE, s)
    gw = GROUP_WIDTH
    return pl.pallas_call(
        functools.partial(body, t=t),
        grid=(b, s // t),
        in_specs=[
            pl.BlockSpec((1, t, gw), lambda bi, qi: (bi, qi, 2 * group)),
            pl.BlockSpec((1, s, gw), lambda bi, qi: (bi, 0, 2 * group + 1)),
            pl.BlockSpec((1, gw, s), lambda bi, qi: (bi, group, 0)),
        ] + extra_specs,
        out_specs=pl.BlockSpec((1, t, gw), lambda bi, qi: (bi, qi, 0)),
        out_shape=jax.ShapeDtypeStruct((b, s, gw), BF16),
        scratch_shapes=[pltpu.VMEM((2, n_chain, t, t), F32),
                        pltpu.VMEM((n_chain, t, t), BF16),
                        pltpu.VMEM((n_chain, LANES, t), F32),
                        pltpu.VMEM((3, n_chain, 1, t), F32)]
                       + list(extra_scratch),
        compiler_params=_params(("arbitrary", "arbitrary")),
        name=name,
    )(qk, qk, vt, *extra_inputs)


def _band_bias(rel_table, t, reach):
    n_keys = (reach + 1) * t
    period = n_keys + t
    d = np.arange(period)
    d = np.where(d < t, d, d - period)
    idx = np.clip(reach * t + d, -MAX_REL_DIST, MAX_REL_DIST) + MAX_REL_DIST
    ext = rel_table.astype(F32)[:, idx] * LOG2E
    h = rel_table.shape[0]
    rows = jnp.tile(ext, (1, n_keys))[:, :n_keys * (period - 1)].reshape(h, n_keys, period - 1)
    bias = rows[:, :, :t]
    kc = (np.arange(n_keys) // CHUNK - (reach * t // CHUNK - BAND_CHUNKS))[:, None]
    qc = (np.arange(t) // CHUNK)[None, :]
    visible = (kc >= qc) & (kc <= qc + BAND_CHUNKS)
    return jnp.where(visible[None], bias, NEG).reshape(h, reach + 1, t, t)


def _mixers(qk, vt, kaug, qaug, rel_table, lam_params, lam_init):
    b, s, _ = qk.shape
    t = min(ATTN_TILE, s)
    assert s % t == 0 and t % CHUNK == 0 and t % LANES == 0
    reach = -(-BAND_CHUNKS * CHUNK // t)
    nh = HEADS_PER_GROUP
    n_aug = kaug.shape[-1]
    o_a = _mixer_call(
        _fox_kernel, qk, vt, 0, nh, [qaug, kaug],
        [pl.BlockSpec((1, t, n_aug), lambda bi, qi: (bi, qi, 0)),
         pl.BlockSpec((1, s, n_aug), lambda bi, qi: (bi, 0, 0))], "fox")
    bias = _band_bias(rel_table, t, reach)
    o_b = _mixer_call(
        functools.partial(_band_kernel, reach=reach), qk, vt, 1, nh, [bias],
        [_resident(bias.shape)], "band")
    q_al, k_al = _alibi_lanes(t)
    fix = _alibi_diag_fix(t)
    o_c = _mixer_call(
        functools.partial(_diff_kernel, lam_init=lam_init), qk, vt, 2, 2 * nh,
        [lam_params, q_al, k_al, fix],
        [_resident(lam_params.shape), _resident(q_al.shape), _resident(k_al.shape),
         _resident(fix.shape)], "diff")
    o_d = _mixer_call(_stick_kernel, qk, vt, 3, nh, [], [], "stick",
                      extra_scratch=[pltpu.VMEM((nh, t, t), F32)])
    return o_a, o_b, o_c, o_d


def _outproj_kernel(x_ref, oa_ref, ob_ref, oc_ref, od_ref, w_ref, o_ref):
    acc = x_ref[...]
    for grp, m_ref in enumerate((oa_ref, ob_ref, oc_ref, od_ref)):
        acc = acc + jnp.dot(m_ref[...], w_ref[grp * GROUP_WIDTH:(grp + 1) * GROUP_WIDTH, :],
                            preferred_element_type=F32)
    o_ref[...] = acc


def _outproj(x2, mixed, w_out):
    n, d = x2.shape
    tm = min(TOKEN_TILE, n)
    gw = GROUP_WIDTH
    mix_spec = pl.BlockSpec((tm, gw), lambda i: (i, 0))
    return pl.pallas_call(
        _outproj_kernel,
        grid=(n // tm,),
        in_specs=[pl.BlockSpec((tm, d), lambda i: (i, 0)), mix_spec, mix_spec, mix_spec, mix_spec,
                  _resident(w_out.shape)],
        out_specs=pl.BlockSpec((tm, d), lambda i: (i, 0)),
        out_shape=jax.ShapeDtypeStruct((n, d), F32),
        compiler_params=_params(("arbitrary",)),
        name="outproj",
    )(x2, *[m.reshape(n, gw) for m in mixed], w_out.astype(BF16))


def _final_norm_kernel(x_ref, g_ref, o_ref):
    o_ref[...] = _rmsnorm(x_ref[...], g_ref[...])


def _final_norm(x2, g):
    n, d = x2.shape
    tm = min(TOKEN_TILE, n)
    return pl.pallas_call(
        _final_norm_kernel,
        grid=(n // tm,),
        in_specs=[pl.BlockSpec((tm, d), lambda i: (i, 0)), _resident((1, d))],
        out_specs=pl.BlockSpec((tm, d), lambda i: (i, 0)),
        out_shape=jax.ShapeDtypeStruct((n, d), F32),
        compiler_params=_params(("arbitrary",)),
        name="final_norm",
    )(x2, g.reshape(1, d))


def _lambda_init(layer_idx):
    return 0.8 - 0.6 * math.exp(-0.3 * layer_idx)


def kernel(x, g_ffn1, ffn1_w_gu, ffn1_w_down, g_mix, w_in, b_f, rel_bias, diff_lambda, w_out,
           g_ffn2, ffn2_w_gu, ffn2_w_down, g_final):
    b, s, d = x.shape
    x2 = x.reshape(b * s, d)
    for layer in range(g_ffn1.shape[0]):
        x2 = _ffn(x2, g_ffn1[layer], ffn1_w_gu[layer], ffn1_w_down[layer])
        qk, vt, kaug, qaug = _inproj(x2.reshape(b, s, d), g_mix[layer], w_in[layer], b_f[layer])
        mixed = _mixers(qk, vt, kaug, qaug, rel_bias[layer], diff_lambda[layer], _lambda_init(layer))
        x2 = _outproj(x2, mixed, w_out[layer])
        x2 = _ffn(x2, g_ffn2[layer], ffn2_w_gu[layer], ffn2_w_down[layer])
    return _final_norm(x2, g_final).reshape(b, s, d)
```

```python
import functools
import math

import numpy as np
import jax
import jax.numpy as jnp
from jax import lax
from jax.experimental import pallas as pl
from jax.experimental.pallas import tpu as pltpu

F32 = jnp.float32
BF16 = jnp.bfloat16

LANES = 128
HEAD_DIM = 64
HEADS_PER_GROUP = 4
GROUP_WIDTH = HEADS_PER_GROUP * HEAD_DIM
N_GROUPS = 4
CHUNK = 64
BAND_CHUNKS = 8
MAX_REL_DIST = 256
DIFF_QK_DIM = HEAD_DIM // 2
FFN_RES = 0.5
RMS_EPS = 1e-6
NEG = -1e30
LOG2E = math.log2(math.e)
N_SPLIT = 3

TOKEN_TILE = 512
ATTN_TILE = 256
FFN_CHUNK = 1408
PREFIX_BLOCK = 256
VMEM_LIMIT = 56 * 1024 * 1024

_NT = (((1,), (1,)), ((), ()))


def _rmsnorm(x, g):
    y = x * lax.rsqrt(jnp.mean(x * x, axis=-1, keepdims=True) + RMS_EPS)
    return y * g


def _log_sigmoid(x):
    return jnp.minimum(x, 0.0) - jnp.log1p(jnp.exp(-jnp.abs(x)))


def _split3(x):
    h1 = x.astype(BF16)
    r1 = x - h1.astype(F32)
    h2 = r1.astype(BF16)
    h3 = (r1 - h2.astype(F32)).astype(BF16)
    return h1, h2, h3


def _split3_const(c):
    parts = []
    for _ in range(N_SPLIT):
        p = float(np.asarray(c, np.float32).astype(jnp.bfloat16).astype(np.float32))
        parts.append(p)
        c = c - p
    return parts


def _params(sem):
    return pltpu.CompilerParams(dimension_semantics=sem, vmem_limit_bytes=VMEM_LIMIT)


def _resident(shape):
    zeros = (0,) * len(shape)
    return pl.BlockSpec(shape, lambda *_: zeros, pipeline_mode=pl.Buffered(1))


def _ffn_kernel(*refs, d_ff, chunk, n_mixed, final_norm):
    x_ref, refs = refs[0], refs[1:]
    x = x_ref[...]
    if n_mixed:
        wout_ref = refs[n_mixed]
        width = wout_ref.shape[0] // n_mixed
        for grp, m_ref in enumerate(refs[:n_mixed]):
            x = x + jnp.dot(m_ref[...], wout_ref[grp * width:(grp + 1) * width, :],
                            preferred_element_type=F32)
        refs = refs[n_mixed + 1:]
    g_ref, wgu_ref, wd_ref = refs[:3]
    o_ref = refs[-1]
    h = _rmsnorm(x, g_ref[...]).astype(BF16)
    acc = jnp.zeros(x.shape, F32)
    for c0 in range(0, d_ff, chunk):
        gate = jnp.dot(h, wgu_ref[:, c0:c0 + chunk], preferred_element_type=F32)
        up = jnp.dot(h, wgu_ref[:, d_ff + c0:d_ff + c0 + chunk], preferred_element_type=F32)
        act = gate * jax.nn.sigmoid(gate) * up
        acc = acc + jnp.dot(act.astype(BF16), wd_ref[c0:c0 + chunk, :], preferred_element_type=F32)
    y = x + FFN_RES * acc
    o_ref[...] = _rmsnorm(y, refs[3][...]) if final_norm else y


def _ffn(x2, g, w_gu, w_down, mixed=(), w_out=None, g_final=None):
    n, d = x2.shape
    d_ff = w_down.shape[0]
    chunk = FFN_CHUNK if d_ff % FFN_CHUNK == 0 else d_ff
    tm = min(TOKEN_TILE, n)
    row_spec = pl.BlockSpec((tm, d), lambda i: (i, 0))
    args, specs = [x2], [row_spec]
    if mixed:
        for m in mixed:
            args.append(m.reshape(n, m.shape[-1]))
            specs.append(pl.BlockSpec((tm, m.shape[-1]), lambda i: (i, 0)))
        args.append(w_out.astype(BF16))
        specs.append(_resident(w_out.shape))
    args += [g.reshape(1, d), w_gu.astype(BF16), w_down.astype(BF16)]
    specs += [_resident((1, d)), _resident((d, 2 * d_ff)), _resident((d_ff, d))]
    if g_final is not None:
        args.append(g_final.reshape(1, d))
        specs.append(_resident((1, d)))
    return pl.pallas_call(
        functools.partial(_ffn_kernel, d_ff=d_ff, chunk=chunk, n_mixed=len(mixed),
                          final_norm=g_final is not None),
        grid=(n // tm,),
        in_specs=specs,
        out_specs=row_spec,
        out_shape=jax.ShapeDtypeStruct((n, d), F32),
        compiler_params=_params(("arbitrary",)),
        name="ffn",
    )(*args)


def _inproj_kernel(x_ref, g_ref, wqk_ref, cs_ref, wvt_ref, wf_ref, bf_ref,
                   qk_ref, vt_ref, kaug_ref, qaug_ref, carry_ref, *, tm, blk):
    @pl.when(pl.program_id(1) == 0)
    def _():
        carry_ref[...] = jnp.zeros_like(carry_ref)

    h = _rmsnorm(x_ref[0], g_ref[...]).astype(BF16)
    qk = jnp.dot(h, wqk_ref[...], preferred_element_type=F32)
    qk_ref[0] = (qk * cs_ref[...]).astype(BF16)
    vt_ref[0] = lax.dot_general(wvt_ref[...], h, _NT, preferred_element_type=F32).astype(BF16)

    ri = lax.broadcasted_iota(jnp.int32, (blk, blk), 0)
    ci = lax.broadcasted_iota(jnp.int32, (blk, blk), 1)
    tril = jnp.where(ci <= ri, 1.0, 0.0).astype(BF16)
    ls = _log_sigmoid(jnp.dot(h, wf_ref[...], preferred_element_type=F32) + bf_ref[...])
    lane = lax.broadcasted_iota(jnp.int32, (1, LANES), 1)
    used = lane < 2 * N_SPLIT * HEADS_PER_GROUP
    slot = lane % (2 * N_SPLIT)
    carry = carry_ref[...]
    for r0 in range(0, tm, blk):
        cum = carry
        for part in _split3(ls[r0:r0 + blk]):
            cum = cum + jnp.dot(tril, part, preferred_element_type=F32)
        carry = cum[blk - 1:blk, :]
        kaug = jnp.where(slot >= N_SPLIT, 1.0, 0.0)
        qaug = jnp.where(slot < N_SPLIT, -1.0, 0.0)
        for i, part in enumerate(_split3(cum * LOG2E)):
            kaug = jnp.where(slot == i, part.astype(F32), kaug)
            qaug = jnp.where(slot == N_SPLIT + i, part.astype(F32), qaug)
        kaug_ref[0, r0:r0 + blk, :] = jnp.where(used, kaug, 0.0).astype(BF16)
        qaug_ref[0, r0:r0 + blk, :] = jnp.where(used, qaug, 0.0).astype(BF16)
    carry_ref[...] = carry


def _inproj(x3, g, w_in, b_f):
    b, s, d = x3.shape
    gw = GROUP_WIDTH
    nf = HEADS_PER_GROUP
    starts = (0, 3 * gw + nf, 6 * gw + nf, 9 * gw + nf)
    w_qk = jnp.concatenate([w_in[:, st:st + 2 * gw] for st in starts], axis=1).astype(BF16)
    w_vt = jnp.concatenate([w_in[:, st + 2 * gw:st + 3 * gw] for st in starts], axis=1).T.astype(BF16)
    rep = 2 * N_SPLIT
    w_f = jnp.pad(jnp.repeat(w_in[:, 3 * gw:3 * gw + nf], rep, axis=1),
                  ((0, 0), (0, LANES - rep * nf))).astype(BF16)
    b_col = jnp.pad(jnp.repeat(b_f, rep), (0, LANES - rep * nf)).reshape(1, LANES)
    width = w_qk.shape[1]
    cs = np.ones((1, width), np.float32)
    for grp in range(N_GROUPS):
        scale = DIFF_QK_DIM ** -0.5 if grp == 2 else HEAD_DIM ** -0.5
        cs[0, 2 * grp * gw:(2 * grp + 1) * gw] = scale * LOG2E
    tm = min(TOKEN_TILE, s)
    blk = min(PREFIX_BLOCK, tm)
    return pl.pallas_call(
        functools.partial(_inproj_kernel, tm=tm, blk=blk),
        grid=(b, s // tm),
        in_specs=[
            pl.BlockSpec((1, tm, d), lambda bi, j: (bi, j, 0)),
            _resident((1, d)),
            _resident((d, width)),
            _resident((1, width)),
            _resident((N_GROUPS * gw, d)),
            _resident((d, LANES)),
            _resident((1, LANES)),
        ],
        out_specs=[
            pl.BlockSpec((1, tm, width), lambda bi, j: (bi, j, 0)),
            pl.BlockSpec((1, N_GROUPS * gw, tm), lambda bi, j: (bi, 0, j)),
            pl.BlockSpec((1, tm, LANES), lambda bi, j: (bi, j, 0)),
            pl.BlockSpec((1, tm, LANES), lambda bi, j: (bi, j, 0)),
        ],
        out_shape=[
            jax.ShapeDtypeStruct((b, s, width), BF16),
            jax.ShapeDtypeStruct((b, N_GROUPS * gw, s), BF16),
            jax.ShapeDtypeStruct((b, s, LANES), BF16),
            jax.ShapeDtypeStruct((b, s, LANES), BF16),
        ],
        scratch_shapes=[pltpu.VMEM((1, LANES), F32)],
        compiler_params=_params(("arbitrary", "arbitrary")),
        name="inproj",
    )(x3, g.reshape(1, d), w_qk, jnp.asarray(cs), w_vt, w_f, b_col)


def _lane_range(lo, width):
    lane = lax.broadcasted_iota(jnp.int32, (1, LANES), 1)
    return (lane >= lo) & (lane < lo + width)


def _masked_queries(q_ref, width):
    out = []
    for pair in range(2):
        q = q_ref[0, :, pair * LANES:(pair + 1) * LANES]
        for lo in range(0, LANES, width):
            out.append(jnp.where(_lane_range(lo, width), q, jnp.zeros_like(q)))
    return out


def _head_scores(k_ref, qs, t, per_head, k_extra=None):
    def scores(kb):
        k0 = pl.multiple_of(kb * t, t)
        out = []
        for c, q in enumerate(qs):
            pair = c // (2 * per_head)
            k = k_ref[0, pl.ds(k0, t), pair * LANES:(pair + 1) * LANES]
            if k_extra is not None:
                k = jnp.concatenate([k, k_extra(c, k0)], axis=1)
            out.append(lax.dot_general(k, q, _NT, preferred_element_type=F32))
        return out
    return scores


def _value_rows(vt, hh):
    ones = jnp.ones((HEAD_DIM, vt.shape[1]), vt.dtype)
    if hh == 0:
        return jnp.concatenate([vt[:HEAD_DIM], ones], axis=0)
    return jnp.concatenate([ones, vt[HEAD_DIM:]], axis=0)


def _head_products(vt_ref, t, per_head, with_ones=True):
    def products(ps, kb):
        k0 = pl.multiple_of(kb * t, t)
        out = []
        for c, p in enumerate(ps):
            h = c // per_head
            vt = vt_ref[0, (h // 2) * LANES:(h // 2 + 1) * LANES, pl.ds(k0, t)]
            if with_ones:
                vt = _value_rows(vt, h % 2)
            out.append(jnp.dot(vt, p, preferred_element_type=F32))
        return out
    return products


_MAX, _ALPHA, _OFFSET = range(3)


def _flash_forward(kb_lo, kb_hi, t, n_chain, scores, adjust, shift_of, products,
                   s_scr, p_scr, acc_scr, st_scr):
    chains = range(n_chain)

    def land(kb, tiles, diag):
        slot = kb & 1
        for c in chains:
            s = adjust(c, tiles[c], kb, diag)
            s_scr[slot, c] = s
            mx = jnp.max(s, axis=0, keepdims=True)
            shift = shift_of(c, kb, diag)
            if shift is not None:
                mx = mx + shift
            m_old = st_scr[_MAX, c]
            m_new = jnp.maximum(m_old, mx)
            st_scr[_MAX, c] = m_new
            st_scr[_ALPHA, c] = jnp.exp2(m_old - m_new)
            st_scr[_OFFSET, c] = m_new if shift is None else m_new - shift

    def weigh(kb):
        slot = kb & 1
        for c in chains:
            r = st_scr[_OFFSET, c]
            for lo in range(0, t, LANES):
                cols = slice(lo, lo + LANES)
                p_scr[c, :, cols] = jnp.exp2(s_scr[slot, c, :, cols] - r[:, cols]).astype(BF16)
        return products([p_scr[c] for c in chains], kb)

    def accumulate(alpha, prod):
        for c in chains:
            acc_scr[c] = alpha[c] * acc_scr[c] + prod[c]

    def step(kb, diag_next):
        alpha = [st_scr[_ALPHA, c] for c in chains]
        nxt = scores(kb + 1)
        prod = weigh(kb)
        land(kb + 1, nxt, diag_next)
        accumulate(alpha, prod)

    for c in chains:
        st_scr[_MAX, c] = jnp.full((1, t), NEG, F32)
        acc_scr[c] = jnp.zeros((LANES, t), F32)

    @pl.when(kb_lo == kb_hi)
    def _():
        land(kb_lo, scores(kb_lo), True)

    @pl.when(kb_lo != kb_hi)
    def _():
        land(kb_lo, scores(kb_lo), False)

    def body(kb, carry):
        step(kb, False)
        return carry

    lax.fori_loop(kb_lo, kb_hi - 1, body, 0)

    @pl.when(kb_hi > kb_lo)
    def _():
        step(kb_hi - 1, True)

    alpha = [st_scr[_ALPHA, c] for c in chains]
    prod = weigh(kb_hi)
    return [alpha[c] * acc_scr[c] + prod[c] for c in chains]


def _normalized(acc, hh):
    if hh == 0:
        return acc[:HEAD_DIM] * (1.0 / acc[HEAD_DIM:HEAD_DIM + 1])
    return acc[HEAD_DIM:] * (1.0 / acc[0:1])


def _store_pair(o_ref, pair, top, bottom):
    o_t = jnp.concatenate([top, bottom], axis=0)
    o_ref[0, :, pair * LANES:(pair + 1) * LANES] = o_t.T.astype(o_ref.dtype)


def _store_normalized(o_ref, acc):
    for pair in range(2):
        _store_pair(o_ref, pair, _normalized(acc[2 * pair], 0), _normalized(acc[2 * pair + 1], 1))


def _tile_iotas(t):
    key = lax.broadcasted_iota(jnp.int32, (t, t), 0)
    qry = lax.broadcasted_iota(jnp.int32, (t, t), 1)
    return key, qry


def _no_shift(c, kb, diag):
    return None


def _fox_kernel(q_ref, k_ref, vt_ref, qaug_ref, kaug_ref, o_ref, s_scr, p_scr, acc_scr, st_scr,
                *, t):
    qi = pl.program_id(1)
    q_aug = qaug_ref[0]
    rep = 2 * N_SPLIT
    qs = [jnp.concatenate([q, jnp.where(_lane_range(rep * h, rep), q_aug, jnp.zeros_like(q_aug))],
                          axis=1)
          for h, q in enumerate(_masked_queries(q_ref, HEAD_DIM))]

    def k_extra(h, k0):
        return kaug_ref[0, pl.ds(k0, t), :]

    def adjust(h, s, kb, diag):
        if not diag:
            return s
        key, qry = _tile_iotas(t)
        return jnp.where(key <= qry, s, NEG)

    acc = _flash_forward(0, qi, t, HEADS_PER_GROUP, _head_scores(k_ref, qs, t, 1, k_extra),
                         adjust, _no_shift, _head_products(vt_ref, t, 1),
                         s_scr, p_scr, acc_scr, st_scr)
    _store_normalized(o_ref, acc)


def _band_kernel(q_ref, k_ref, vt_ref, bias_ref, o_ref, s_scr, p_scr, acc_scr, st_scr,
                 *, t, reach):
    qi = pl.program_id(1)
    qs = _masked_queries(q_ref, HEAD_DIM)

    def adjust(h, s, kb, diag):
        return s + bias_ref[h, kb - qi + reach]

    acc = _flash_forward(jnp.maximum(qi - reach, 0), qi, t, HEADS_PER_GROUP,
                         _head_scores(k_ref, qs, t, 1), adjust, _no_shift,
                         _head_products(vt_ref, t, 1), s_scr, p_scr, acc_scr, st_scr)
    _store_normalized(o_ref, acc)


def _alibi_lanes(t):
    q_aug = np.zeros((HEADS_PER_GROUP, t, LANES), np.float32)
    k_aug = np.zeros((HEADS_PER_GROUP, t, LANES), np.float32)
    off = np.arange(t, dtype=np.float32)
    for h in range(HEADS_PER_GROUP):
        parts = _split3_const(_alibi_slope(h) * LOG2E)
        for i, c in enumerate(parts):
            q_aug[h, :, i] = off
            k_aug[h, :, i] = -c
            q_aug[h, :, N_SPLIT + i] = c
            k_aug[h, :, N_SPLIT + i] = off
    return jnp.asarray(q_aug, BF16), jnp.asarray(k_aug, BF16)


def _alibi_slope(h):
    return 2.0 ** (-8.0 * (h + 1) / HEADS_PER_GROUP)


def _alibi_diag_fix(t):
    key = np.arange(t)[:, None]
    qry = np.arange(t)[None, :]
    dist = np.minimum(qry - key, 0).astype(np.float32)
    fix = np.stack([dist * np.float32(2.0 * _alibi_slope(h) * LOG2E) for h in range(HEADS_PER_GROUP)])
    return jnp.asarray(np.where((key // CHUNK <= qry // CHUNK)[None], fix, np.float32(NEG)))


def _diff_kernel(q_ref, k_ref, vt_ref, lam_ref, qal_ref, kal_ref, fix_ref, o_ref,
                 s_scr, p_scr, acc_scr, st_scr, *, t, lam_init):
    qi = pl.program_id(1)
    lp = lam_ref[...]
    lam = (jnp.exp(jnp.sum(lp[0:1] * lp[1:2], axis=-1, keepdims=True))
           - jnp.exp(jnp.sum(lp[2:3] * lp[3:4], axis=-1, keepdims=True)) + lam_init)
    qs = [jnp.concatenate([q, qal_ref[c // 2]], axis=1)
          for c, q in enumerate(_masked_queries(q_ref, DIFF_QK_DIM))]

    def k_extra(c, k0):
        return kal_ref[c // 2]

    def adjust(c, s, kb, diag):
        return s + fix_ref[c // 2] if diag else s

    def shift_of(c, kb, diag):
        if diag:
            return None
        return (-_alibi_slope(c // 2) * LOG2E * t) * (qi - kb).astype(F32)

    n_chain = 2 * HEADS_PER_GROUP
    acc = _flash_forward(0, qi, t, n_chain, _head_scores(k_ref, qs, t, 2, k_extra), adjust,
                         shift_of, _head_products(vt_ref, t, 2), s_scr, p_scr, acc_scr, st_scr)
    for pair in range(2):
        halves = []
        for hh in range(2):
            h = 2 * pair + hh
            o_h = _normalized(acc[2 * h], hh) - lam * _normalized(acc[2 * h + 1], hh)
            ms = jnp.mean(o_h * o_h, axis=0, keepdims=True)
            halves.append(o_h * lax.rsqrt(ms + RMS_EPS) * (1.0 - lam_init))
        _store_pair(o_ref, pair, halves[0], halves[1])


def _softplus2(z):
    return jnp.maximum(z, 0.0) + jnp.log(1.0 + jnp.exp2(-jnp.abs(z))) * LOG2E


def _stick_kernel(q_ref, k_ref, vt_ref, o_ref, z_scr, w_scr, acc_scr, st_scr, wi_scr, *, t):
    qi = pl.program_id(1)
    qs = _masked_queries(q_ref, HEAD_DIM)
    chains = range(HEADS_PER_GROUP)
    scores = _head_scores(k_ref, qs, t, 1)
    products = _head_products(vt_ref, t, 1, with_ones=False)
    tail_scr = st_scr.at[0]

    def land(j, tiles, diag):
        slot = j & 1
        key, qry = _tile_iotas(t)
        suffix_ones = jnp.where(qry >= key, 1.0, 0.0).astype(BF16)
        within = []
        for c in chains:
            z = tiles[c]
            rest = -_softplus2(z)
            if diag:
                rest = jnp.where(key < qry, rest, 0.0)
                z = jnp.where(key < qry, z, NEG)
            z_scr[slot, c] = z
            hi = rest.astype(BF16)
            lo = (rest - hi.astype(F32)).astype(BF16)
            within.append(jnp.dot(suffix_ones, hi, preferred_element_type=F32)
                          + jnp.dot(suffix_ones, lo, preferred_element_type=F32))
        return within

    def weigh(j):
        slot = j & 1
        for c in chains:
            tail = tail_scr[c]
            for lo in range(0, t, LANES):
                cols = slice(lo, lo + LANES)
                logw = z_scr[slot, c, :, cols] + wi_scr[c, :, cols] + tail[:, cols]
                w_scr[c, :, cols] = jnp.exp2(logw).astype(BF16)
            tail_scr[c] = tail + wi_scr[c, 0:1, :]
        return products([w_scr[c] for c in chains], qi - j)

    def collect(prod, within):
        for c in chains:
            acc_scr[c] = acc_scr[c] + prod[c]
            wi_scr[c] = within[c]

    def body(j, carry):
        nxt = scores(qi - j - 1)
        prod = weigh(j)
        collect(prod, land(j + 1, nxt, False))
        return carry

    for c, wi in enumerate(land(0, scores(qi), True)):
        tail_scr[c] = jnp.zeros((1, t), F32)
        acc_scr[c] = jnp.zeros((LANES, t), F32)
        wi_scr[c] = wi
    lax.fori_loop(0, qi, body, 0)
    prod = weigh(qi)
    acc = [acc_scr[c] + prod[c] for c in chains]
    for pair in range(2):
        _store_pair(o_ref, pair, acc[2 * pair][:HEAD_DIM], acc[2 * pair + 1][HEAD_DIM:])


def _mixer_call(body, qk, vt, group, n_chain, extra_inputs, extra_specs, name, extra_scratch=()):
    b, s, _ = qk.shape
    t = min(ATTN_TILE, s)
    gw = GROUP_WIDTH
    return pl.pallas_call(
        functools.partial(body, t=t),
        grid=(b, s // t),
        in_specs=[
            pl.BlockSpec((1, t, gw), lambda bi, qi: (bi, qi, 2 * group)),
            pl.BlockSpec((1, s, gw), lambda bi, qi: (bi, 0, 2 * group + 1)),
            pl.BlockSpec((1, gw, s), lambda bi, qi: (bi, group, 0)),
        ] + extra_specs,
        out_specs=pl.BlockSpec((1, t, gw), lambda bi, qi: (bi, qi, 0)),
        out_shape=jax.ShapeDtypeStruct((b, s, gw), BF16),
        scratch_shapes=[pltpu.VMEM((2, n_chain, t, t), F32),
                        pltpu.VMEM((n_chain, t, t), BF16),
                        pltpu.VMEM((n_chain, LANES, t), F32),
                        pltpu.VMEM((3, n_chain, 1, t), F32)]
                       + list(extra_scratch),
        compiler_params=_params(("arbitrary", "arbitrary")),
        name=name,
    )(qk, qk, vt, *extra_inputs)


def _band_bias(rel_table, t, reach):
    n_keys = (reach + 1) * t
    period = n_keys + t
    d = np.arange(period)
    d = np.where(d < t, d, d - period)
    idx = np.clip(reach * t + d, -MAX_REL_DIST, MAX_REL_DIST) + MAX_REL_DIST
    ext = rel_table.astype(F32)[:, idx] * LOG2E
    h = rel_table.shape[0]
    rows = jnp.tile(ext, (1, n_keys))[:, :n_keys * (period - 1)].reshape(h, n_keys, period - 1)
    bias = rows[:, :, :t]
    kc = (np.arange(n_keys) // CHUNK - (reach * t // CHUNK - BAND_CHUNKS))[:, None]
    qc = (np.arange(t) // CHUNK)[None, :]
    visible = (kc >= qc) & (kc <= qc + BAND_CHUNKS)
    return jnp.where(visible[None], bias, NEG).reshape(h, reach + 1, t, t)


def _mixers(qk, vt, kaug, qaug, rel_table, lam_params, lam_init):
    b, s, _ = qk.shape
    t = min(ATTN_TILE, s)
    assert s % t == 0 and t % CHUNK == 0 and t % LANES == 0
    reach = -(-BAND_CHUNKS * CHUNK // t)
    nh = HEADS_PER_GROUP
    o_a = _mixer_call(
        _fox_kernel, qk, vt, 0, nh, [qaug, kaug],
        [pl.BlockSpec((1, t, LANES), lambda bi, qi: (bi, qi, 0)),
         pl.BlockSpec((1, s, LANES), lambda bi, qi: (bi, 0, 0))], "fox")
    bias = _band_bias(rel_table, t, reach)
    o_b = _mixer_call(
        functools.partial(_band_kernel, reach=reach), qk, vt, 1, nh, [bias],
        [_resident(bias.shape)], "band")
    q_al, k_al = _alibi_lanes(t)
    fix = _alibi_diag_fix(t)
    o_c = _mixer_call(
        functools.partial(_diff_kernel, lam_init=lam_init), qk, vt, 2, 2 * nh,
        [lam_params, q_al, k_al, fix],
        [_resident(lam_params.shape), _resident(q_al.shape), _resident(k_al.shape),
         _resident(fix.shape)], "diff")
    o_d = _mixer_call(_stick_kernel, qk, vt, 3, nh, [], [], "stick",
                      extra_scratch=[pltpu.VMEM((nh, t, t), F32)])
    return o_a, o_b, o_c, o_d


def _lambda_init(layer_idx):
    return 0.8 - 0.6 * math.exp(-0.3 * layer_idx)


def kernel(x, g_ffn1, ffn1_w_gu, ffn1_w_down, g_mix, w_in, b_f, rel_bias, diff_lambda, w_out,
           g_ffn2, ffn2_w_gu, ffn2_w_down, g_final):
    b, s, d = x.shape
    depth = g_ffn1.shape[0]
    x2 = x.reshape(b * s, d)
    for layer in range(depth):
        x2 = _ffn(x2, g_ffn1[layer], ffn1_w_gu[layer], ffn1_w_down[layer])
        qk, vt, kaug, qaug = _inproj(x2.reshape(b, s, d), g_mix[layer], w_in[layer], b_f[layer])
        mixed = _mixers(qk, vt, kaug, qaug, rel_bias[layer], diff_lambda[layer], _lambda_init(layer))
        x2 = _ffn(x2, g_ffn2[layer], ffn2_w_gu[layer], ffn2_w_down[layer], mixed, w_out[layer],
                  g_final if layer == depth - 1 else None)
    return x2.reshape(b, s, d)
```

```python
import functools
import math

import numpy as np
import jax
import jax.numpy as jnp
from jax import lax
from jax.experimental import pallas as pl
from jax.experimental.pallas import tpu as pltpu

F32 = jnp.float32
BF16 = jnp.bfloat16

LANES = 128
HEAD_DIM = 64
HEADS_PER_GROUP = 4
GROUP_WIDTH = HEADS_PER_GROUP * HEAD_DIM
N_GROUPS = 4
CHUNK = 64
BAND_CHUNKS = 8
MAX_REL_DIST = 256
DIFF_QK_DIM = HEAD_DIM // 2
FFN_RES = 0.5
RMS_EPS = 1e-6
NEG = -1e30
LOG2E = math.log2(math.e)
N_SPLIT = 3

TOKEN_TILE = 512
ATTN_TILE = 256
FFN_CHUNK = 1408
PREFIX_BLOCK = 256
VMEM_LIMIT = 56 * 1024 * 1024

_NT = (((1,), (1,)), ((), ()))


def _rmsnorm(x, g):
    y = x * lax.rsqrt(jnp.mean(x * x, axis=-1, keepdims=True) + RMS_EPS)
    return y * g


def _log_sigmoid(x):
    return jnp.minimum(x, 0.0) - jnp.log1p(jnp.exp(-jnp.abs(x)))


def _split3(x):
    h1 = x.astype(BF16)
    r1 = x - h1.astype(F32)
    h2 = r1.astype(BF16)
    h3 = (r1 - h2.astype(F32)).astype(BF16)
    return h1, h2, h3


def _split3_const(c):
    parts = []
    for _ in range(N_SPLIT):
        p = float(np.asarray(c, np.float32).astype(jnp.bfloat16).astype(np.float32))
        parts.append(p)
        c = c - p
    return parts


def _params(sem):
    return pltpu.CompilerParams(dimension_semantics=sem, vmem_limit_bytes=VMEM_LIMIT)


def _resident(shape):
    zeros = (0,) * len(shape)
    return pl.BlockSpec(shape, lambda *_: zeros, pipeline_mode=pl.Buffered(1))


def _ffn_kernel(*refs, d_ff, chunk, n_mixed, final_norm):
    x_ref, refs = refs[0], refs[1:]
    x = x_ref[...]
    if n_mixed:
        wout_ref = refs[n_mixed]
        width = wout_ref.shape[0] // n_mixed
        for grp, m_ref in enumerate(refs[:n_mixed]):
            x = x + jnp.dot(m_ref[...], wout_ref[grp * width:(grp + 1) * width, :],
                            preferred_element_type=F32)
        refs = refs[n_mixed + 1:]
    g_ref, wgu_ref, wd_ref = refs[:3]
    o_ref = refs[-1]
    h = _rmsnorm(x, g_ref[...]).astype(BF16)
    acc = jnp.zeros(x.shape, F32)
    for c0 in range(0, d_ff, chunk):
        gate = jnp.dot(h, wgu_ref[:, c0:c0 + chunk], preferred_element_type=F32)
        up = jnp.dot(h, wgu_ref[:, d_ff + c0:d_ff + c0 + chunk], preferred_element_type=F32)
        act = gate * jax.nn.sigmoid(gate) * up
        acc = acc + jnp.dot(act.astype(BF16), wd_ref[c0:c0 + chunk, :], preferred_element_type=F32)
    y = x + FFN_RES * acc
    o_ref[...] = _rmsnorm(y, refs[3][...]) if final_norm else y


def _ffn(x2, g, w_gu, w_down, mixed=(), w_out=None, g_final=None):
    n, d = x2.shape
    d_ff = w_down.shape[0]
    chunk = FFN_CHUNK if d_ff % FFN_CHUNK == 0 else d_ff
    tm = min(TOKEN_TILE, n)
    row_spec = pl.BlockSpec((tm, d), lambda i: (i, 0))
    args, specs = [x2], [row_spec]
    if mixed:
        for m in mixed:
            args.append(m.reshape(n, m.shape[-1]))
            specs.append(pl.BlockSpec((tm, m.shape[-1]), lambda i: (i, 0)))
        args.append(w_out.astype(BF16))
        specs.append(_resident(w_out.shape))
    args += [g.reshape(1, d), w_gu.astype(BF16), w_down.astype(BF16)]
    specs += [_resident((1, d)), _resident((d, 2 * d_ff)), _resident((d_ff, d))]
    if g_final is not None:
        args.append(g_final.reshape(1, d))
        specs.append(_resident((1, d)))
    return pl.pallas_call(
        functools.partial(_ffn_kernel, d_ff=d_ff, chunk=chunk, n_mixed=len(mixed),
                          final_norm=g_final is not None),
        grid=(n // tm,),
        in_specs=specs,
        out_specs=row_spec,
        out_shape=jax.ShapeDtypeStruct((n, d), F32),
        compiler_params=_params(("arbitrary",)),
        name="ffn",
    )(*args)


def _inproj_kernel(x_ref, g_ref, wqk_ref, cs_ref, wvt_ref, wf_ref, bf_ref,
                   qk_ref, vt_ref, kaug_ref, qaug_ref, carry_ref, *, tm, blk):
    @pl.when(pl.program_id(1) == 0)
    def _():
        carry_ref[...] = jnp.zeros_like(carry_ref)

    h = _rmsnorm(x_ref[0], g_ref[...]).astype(BF16)
    qk = jnp.dot(h, wqk_ref[...], preferred_element_type=F32)
    qk_ref[0] = (qk * cs_ref[...]).astype(BF16)
    vt_ref[0] = lax.dot_general(wvt_ref[...], h, _NT, preferred_element_type=F32).astype(BF16)

    ri = lax.broadcasted_iota(jnp.int32, (blk, blk), 0)
    ci = lax.broadcasted_iota(jnp.int32, (blk, blk), 1)
    tril = jnp.where(ci <= ri, 1.0, 0.0).astype(BF16)
    ls = _log_sigmoid(jnp.dot(h, wf_ref[...], preferred_element_type=F32) + bf_ref[...])
    lane = lax.broadcasted_iota(jnp.int32, (1, LANES), 1)
    used = lane < 2 * N_SPLIT * HEADS_PER_GROUP
    slot = lane % (2 * N_SPLIT)
    carry = carry_ref[...]
    for r0 in range(0, tm, blk):
        cum = carry
        for part in _split3(ls[r0:r0 + blk]):
            cum = cum + jnp.dot(tril, part, preferred_element_type=F32)
        carry = cum[blk - 1:blk, :]
        kaug = jnp.where(slot >= N_SPLIT, 1.0, 0.0)
        qaug = jnp.where(slot < N_SPLIT, -1.0, 0.0)
        for i, part in enumerate(_split3(cum * LOG2E)):
            kaug = jnp.where(slot == i, part.astype(F32), kaug)
            qaug = jnp.where(slot == N_SPLIT + i, part.astype(F32), qaug)
        kaug_ref[0, r0:r0 + blk, :] = jnp.where(used, kaug, 0.0).astype(BF16)
        qaug_ref[0, r0:r0 + blk, :] = jnp.where(used, qaug, 0.0).astype(BF16)
    carry_ref[...] = carry


def _inproj(x3, g, w_in, b_f):
    b, s, d = x3.shape
    gw = GROUP_WIDTH
    nf = HEADS_PER_GROUP
    starts = (0, 3 * gw + nf, 6 * gw + nf, 9 * gw + nf)
    w_qk = jnp.concatenate([w_in[:, st:st + 2 * gw] for st in starts], axis=1).astype(BF16)
    w_vt = jnp.concatenate([w_in[:, st + 2 * gw:st + 3 * gw] for st in starts], axis=1).T.astype(BF16)
    rep = 2 * N_SPLIT
    w_f = jnp.pad(jnp.repeat(w_in[:, 3 * gw:3 * gw + nf], rep, axis=1),
                  ((0, 0), (0, LANES - rep * nf))).astype(BF16)
    b_col = jnp.pad(jnp.repeat(b_f, rep), (0, LANES - rep * nf)).reshape(1, LANES)
    width = w_qk.shape[1]
    cs = np.ones((1, width), np.float32)
    for grp in range(N_GROUPS):
        scale = DIFF_QK_DIM ** -0.5 if grp == 2 else HEAD_DIM ** -0.5
        cs[0, 2 * grp * gw:(2 * grp + 1) * gw] = scale * LOG2E
    tm = min(TOKEN_TILE, s)
    blk = min(PREFIX_BLOCK, tm)
    return pl.pallas_call(
        functools.partial(_inproj_kernel, tm=tm, blk=blk),
        grid=(b, s // tm),
        in_specs=[
            pl.BlockSpec((1, tm, d), lambda bi, j: (bi, j, 0)),
            _resident((1, d)),
            _resident((d, width)),
            _resident((1, width)),
            _resident((N_GROUPS * gw, d)),
            _resident((d, LANES)),
            _resident((1, LANES)),
        ],
        out_specs=[
            pl.BlockSpec((1, tm, width), lambda bi, j: (bi, j, 0)),
            pl.BlockSpec((1, N_GROUPS * gw, tm), lambda bi, j: (bi, 0, j)),
            pl.BlockSpec((1, tm, LANES), lambda bi, j: (bi, j, 0)),
            pl.BlockSpec((1, tm, LANES), lambda bi, j: (bi, j, 0)),
        ],
        out_shape=[
            jax.ShapeDtypeStruct((b, s, width), BF16),
            jax.ShapeDtypeStruct((b, N_GROUPS * gw, s), BF16),
            jax.ShapeDtypeStruct((b, s, LANES), BF16),
            jax.ShapeDtypeStruct((b, s, LANES), BF16),
        ],
        scratch_shapes=[pltpu.VMEM((1, LANES), F32)],
        compiler_params=_params(("arbitrary", "arbitrary")),
        name="inproj",
    )(x3, g.reshape(1, d), w_qk, jnp.asarray(cs), w_vt, w_f, b_col)


def _lane_range(lo, width):
    lane = lax.broadcasted_iota(jnp.int32, (1, LANES), 1)
    return (lane >= lo) & (lane < lo + width)


def _query_rows(ref, q_tile, t):
    return ref[0, pl.ds(pl.multiple_of(q_tile * t, t), t), :]


def _masked_queries(q, width):
    out = []
    for pair in range(2):
        q_pair = q[:, pair * LANES:(pair + 1) * LANES]
        for lo in range(0, LANES, width):
            out.append(jnp.where(_lane_range(lo, width), q_pair, jnp.zeros_like(q_pair)))
    return out


def _head_scores(k_ref, qs, t, per_head, k_extra=None):
    def scores(kb):
        k0 = pl.multiple_of(kb * t, t)
        out = []
        for c, q in enumerate(qs):
            pair = c // (2 * per_head)
            k = k_ref[0, pl.ds(k0, t), pair * LANES:(pair + 1) * LANES]
            if k_extra is not None:
                k = jnp.concatenate([k, k_extra(c, k0)], axis=1)
            out.append(lax.dot_general(k, q, _NT, preferred_element_type=F32))
        return out
    return scores


def _value_rows(vt, hh):
    ones = jnp.ones((HEAD_DIM, vt.shape[1]), vt.dtype)
    if hh == 0:
        return jnp.concatenate([vt[:HEAD_DIM], ones], axis=0)
    return jnp.concatenate([ones, vt[HEAD_DIM:]], axis=0)


def _head_products(vt_ref, t, per_head, with_ones=True):
    def products(ps, kb):
        k0 = pl.multiple_of(kb * t, t)
        out = []
        for c, p in enumerate(ps):
            h = c // per_head
            vt = vt_ref[0, (h // 2) * LANES:(h // 2 + 1) * LANES, pl.ds(k0, t)]
            if with_ones:
                vt = _value_rows(vt, h % 2)
            out.append(jnp.dot(vt, p, preferred_element_type=F32))
        return out
    return products


_MAX, _ALPHA, _OFFSET = range(3)


def _flash_forward(qi, n_q, t, n_chain, tile_ops, products, s_scr, p_scr, acc_scr, st_scr):
    chains = range(n_chain)
    kb_lo, kb_hi, scores, adjust, shift_of = tile_ops(qi)

    def land(kb, tiles, diag, adjust=adjust, shift_of=shift_of):
        slot = kb & 1
        for c in chains:
            s = adjust(c, tiles[c], kb, diag)
            s_scr[slot, c] = s
            mx = jnp.max(s, axis=0, keepdims=True)
            shift = shift_of(c, kb, diag)
            if shift is not None:
                mx = mx + shift
            m_old = st_scr[_MAX, c]
            m_new = jnp.maximum(m_old, mx)
            st_scr[_MAX, c] = m_new
            st_scr[_ALPHA, c] = jnp.exp2(m_old - m_new)
            st_scr[_OFFSET, c] = m_new if shift is None else m_new - shift

    def weigh(kb):
        slot = kb & 1
        for c in chains:
            r = st_scr[_OFFSET, c]
            for lo in range(0, t, LANES):
                cols = slice(lo, lo + LANES)
                p_scr[c, :, cols] = jnp.exp2(s_scr[slot, c, :, cols] - r[:, cols]).astype(BF16)
        return products([p_scr[c] for c in chains], kb)

    def accumulate(alpha, prod):
        for c in chains:
            acc_scr[c] = alpha[c] * acc_scr[c] + prod[c]

    def step(kb, diag_next):
        alpha = [st_scr[_ALPHA, c] for c in chains]
        nxt = scores(kb + 1)
        prod = weigh(kb)
        land(kb + 1, nxt, diag_next)
        accumulate(alpha, prod)

    def reset_stats():
        for c in chains:
            st_scr[_MAX, c] = jnp.full((1, t), NEG, F32)

    for c in chains:
        acc_scr[c] = jnp.zeros((LANES, t), F32)

    @pl.when(qi == 0)
    def _():
        reset_stats()
        land(kb_lo, scores(kb_lo), True)

    def body(kb, carry):
        step(kb, False)
        return carry

    lax.fori_loop(kb_lo, kb_hi - 1, body, 0)

    @pl.when(kb_hi > kb_lo)
    def _():
        step(kb_hi - 1, True)

    alpha = [st_scr[_ALPHA, c] for c in chains]
    prod = weigh(kb_hi)
    nxt_lo, _, nxt_scores, nxt_adjust, nxt_shift = tile_ops(jnp.minimum(qi + 1, n_q - 1))
    reset_stats()
    land(nxt_lo, nxt_scores(nxt_lo), False, nxt_adjust, nxt_shift)
    return [alpha[c] * acc_scr[c] + prod[c] for c in chains]


def _normalized(acc, hh):
    if hh == 0:
        return acc[:HEAD_DIM] * (1.0 / acc[HEAD_DIM:HEAD_DIM + 1])
    return acc[HEAD_DIM:] * (1.0 / acc[0:1])


def _store_pair(o_ref, pair, top, bottom):
    o_t = jnp.concatenate([top, bottom], axis=0)
    o_ref[0, :, pair * LANES:(pair + 1) * LANES] = o_t.T.astype(o_ref.dtype)


def _store_normalized(o_ref, acc):
    for pair in range(2):
        _store_pair(o_ref, pair, _normalized(acc[2 * pair], 0), _normalized(acc[2 * pair + 1], 1))


def _tile_iotas(t):
    key = lax.broadcasted_iota(jnp.int32, (t, t), 0)
    qry = lax.broadcasted_iota(jnp.int32, (t, t), 1)
    return key, qry


def _no_shift(c, kb, diag):
    return None


def _fox_kernel(q_ref, k_ref, vt_ref, qaug_ref, kaug_ref, o_ref, s_scr, p_scr, acc_scr, st_scr,
                *, t):
    rep = 2 * N_SPLIT

    def k_extra(h, k0):
        return kaug_ref[0, pl.ds(k0, t), :]

    def adjust(h, s, kb, diag):
        if not diag:
            return s
        key, qry = _tile_iotas(t)
        return jnp.where(key <= qry, s, NEG)

    def tile_ops(q_tile):
        q_aug = _query_rows(qaug_ref, q_tile, t)
        qs = [jnp.concatenate(
                  [q, jnp.where(_lane_range(rep * h, rep), q_aug, jnp.zeros_like(q_aug))], axis=1)
              for h, q in enumerate(_masked_queries(_query_rows(q_ref, q_tile, t), HEAD_DIM))]
        return 0, q_tile, _head_scores(k_ref, qs, t, 1, k_extra), adjust, _no_shift

    acc = _flash_forward(pl.program_id(1), pl.num_programs(1), t, HEADS_PER_GROUP, tile_ops,
                         _head_products(vt_ref, t, 1), s_scr, p_scr, acc_scr, st_scr)
    _store_normalized(o_ref, acc)


def _band_kernel(q_ref, k_ref, vt_ref, bias_ref, o_ref, s_scr, p_scr, acc_scr, st_scr,
                 *, t, reach):
    def tile_ops(q_tile):
        qs = _masked_queries(_query_rows(q_ref, q_tile, t), HEAD_DIM)

        def adjust(h, s, kb, diag):
            return s + bias_ref[h, kb - q_tile + reach]

        return (jnp.maximum(q_tile - reach, 0), q_tile, _head_scores(k_ref, qs, t, 1), adjust,
                _no_shift)

    acc = _flash_forward(pl.program_id(1), pl.num_programs(1), t, HEADS_PER_GROUP, tile_ops,
                         _head_products(vt_ref, t, 1), s_scr, p_scr, acc_scr, st_scr)
    _store_normalized(o_ref, acc)


def _alibi_lanes(t):
    q_aug = np.zeros((HEADS_PER_GROUP, t, LANES), np.float32)
    k_aug = np.zeros((HEADS_PER_GROUP, t, LANES), np.float32)
    off = np.arange(t, dtype=np.float32)
    for h in range(HEADS_PER_GROUP):
        parts = _split3_const(_alibi_slope(h) * LOG2E)
        for i, c in enumerate(parts):
            q_aug[h, :, i] = off
            k_aug[h, :, i] = -c
            q_aug[h, :, N_SPLIT + i] = c
            k_aug[h, :, N_SPLIT + i] = off
    return jnp.asarray(q_aug, BF16), jnp.asarray(k_aug, BF16)


def _alibi_slope(h):
    return 2.0 ** (-8.0 * (h + 1) / HEADS_PER_GROUP)


def _alibi_diag_fix(t):
    key = np.arange(t)[:, None]
    qry = np.arange(t)[None, :]
    dist = np.minimum(qry - key, 0).astype(np.float32)
    fix = np.stack([dist * np.float32(2.0 * _alibi_slope(h) * LOG2E) for h in range(HEADS_PER_GROUP)])
    return jnp.asarray(np.where((key // CHUNK <= qry // CHUNK)[None], fix, np.float32(NEG)))


def _diff_kernel(q_ref, k_ref, vt_ref, lam_ref, qal_ref, kal_ref, fix_ref, o_ref,
                 s_scr, p_scr, acc_scr, st_scr, *, t, lam_init):
    qi = pl.program_id(1)
    lp = lam_ref[...]
    lam = (jnp.exp(jnp.sum(lp[0:1] * lp[1:2], axis=-1, keepdims=True))
           - jnp.exp(jnp.sum(lp[2:3] * lp[3:4], axis=-1, keepdims=True)) + lam_init)
    def k_extra(c, k0):
        return kal_ref[c // 2]

    def adjust(c, s, kb, diag):
        return s + fix_ref[c // 2] if diag else s

    def tile_ops(q_tile):
        qs = [jnp.concatenate([q, qal_ref[c // 2]], axis=1)
              for c, q in enumerate(_masked_queries(_query_rows(q_ref, q_tile, t), DIFF_QK_DIM))]

        def shift_of(c, kb, diag):
            if diag:
                return None
            return (-_alibi_slope(c // 2) * LOG2E * t) * (q_tile - kb).astype(F32)

        return 0, q_tile, _head_scores(k_ref, qs, t, 2, k_extra), adjust, shift_of

    n_chain = 2 * HEADS_PER_GROUP
    acc = _flash_forward(qi, pl.num_programs(1), t, n_chain, tile_ops,
                         _head_products(vt_ref, t, 2), s_scr, p_scr, acc_scr, st_scr)
    for pair in range(2):
        halves = []
        for hh in range(2):
            h = 2 * pair + hh
            o_h = _normalized(acc[2 * h], hh) - lam * _normalized(acc[2 * h + 1], hh)
            ms = jnp.mean(o_h * o_h, axis=0, keepdims=True)
            halves.append(o_h * lax.rsqrt(ms + RMS_EPS) * (1.0 - lam_init))
        _store_pair(o_ref, pair, halves[0], halves[1])


def _softplus2(z):
    return jnp.maximum(z, 0.0) + jnp.log(1.0 + jnp.exp2(-jnp.abs(z))) * LOG2E


def _stick_kernel(q_ref, k_ref, vt_ref, o_ref, z_scr, w_scr, acc_scr, st_scr, wi_scr, *, t):
    qi = pl.program_id(1)
    chains = range(HEADS_PER_GROUP)
    products = _head_products(vt_ref, t, 1, with_ones=False)
    tail_scr = st_scr.at[0]

    def tile_scores(q_tile):
        return _head_scores(k_ref, _masked_queries(_query_rows(q_ref, q_tile, t), HEAD_DIM), t, 1)

    scores = tile_scores(qi)

    def land(j, tiles, diag):
        slot = j & 1
        key, qry = _tile_iotas(t)
        suffix_ones = jnp.where(qry >= key, 1.0, 0.0).astype(BF16)
        within = []
        for c in chains:
            z = tiles[c]
            rest = -_softplus2(z)
            if diag:
                rest = jnp.where(key < qry, rest, 0.0)
                z = jnp.where(key < qry, z, NEG)
            z_scr[slot, c] = z
            hi = rest.astype(BF16)
            lo = (rest - hi.astype(F32)).astype(BF16)
            within.append(jnp.dot(suffix_ones, hi, preferred_element_type=F32)
                          + jnp.dot(suffix_ones, lo, preferred_element_type=F32))
        return within

    def weigh(j):
        slot = j & 1
        for c in chains:
            tail = tail_scr[c]
            for lo in range(0, t, LANES):
                cols = slice(lo, lo + LANES)
                logw = z_scr[slot, c, :, cols] + wi_scr[c, :, cols] + tail[:, cols]
                w_scr[c, :, cols] = jnp.exp2(logw).astype(BF16)
            tail_scr[c] = tail + wi_scr[c, 0:1, :]
        return products([w_scr[c] for c in chains], qi - j)

    def collect(prod, within):
        for c in chains:
            acc_scr[c] = acc_scr[c] + prod[c]
            wi_scr[c] = within[c]

    def body(j, carry):
        nxt = scores(qi - j - 1)
        prod = weigh(j)
        collect(prod, land(j + 1, nxt, False))
        return carry

    def start(q_tile, tile_fn):
        for c, wi in enumerate(land(0, tile_fn(q_tile), True)):
            tail_scr[c] = jnp.zeros((1, t), F32)
            wi_scr[c] = wi

    for c in chains:
        acc_scr[c] = jnp.zeros((LANES, t), F32)

    @pl.when(qi == 0)
    def _():
        start(qi, scores)

    lax.fori_loop(0, qi, body, 0)
    prod = weigh(qi)
    nxt = jnp.minimum(qi + 1, pl.num_programs(1) - 1)
    start(nxt, tile_scores(nxt))
    acc = [acc_scr[c] + prod[c] for c in chains]
    for pair in range(2):
        _store_pair(o_ref, pair, acc[2 * pair][:HEAD_DIM], acc[2 * pair + 1][HEAD_DIM:])


def _mixer_call(body, qk, vt, group, n_chain, extra_inputs, extra_specs, name, extra_scratch=()):
    b, s, _ = qk.shape
    t = min(ATTN_TILE, s)
    gw = GROUP_WIDTH
    return pl.pallas_call(
        functools.partial(body, t=t),
        grid=(b, s // t),
        in_specs=[
            pl.BlockSpec((1, s, gw), lambda bi, qi: (bi, 0, 2 * group)),
            pl.BlockSpec((1, s, gw), lambda bi, qi: (bi, 0, 2 * group + 1)),
            pl.BlockSpec((1, gw, s), lambda bi, qi: (bi, group, 0)),
        ] + extra_specs,
        out_specs=pl.BlockSpec((1, t, gw), lambda bi, qi: (bi, qi, 0)),
        out_shape=jax.ShapeDtypeStruct((b, s, gw), BF16),
        scratch_shapes=[pltpu.VMEM((2, n_chain, t, t), F32),
                        pltpu.VMEM((n_chain, t, t), BF16),
                        pltpu.VMEM((n_chain, LANES, t), F32),
                        pltpu.VMEM((3, n_chain, 1, t), F32)]
                       + list(extra_scratch),
        compiler_params=_params(("arbitrary", "arbitrary")),
        name=name,
    )(qk, qk, vt, *extra_inputs)


def _band_bias(rel_table, t, reach):
    n_keys = (reach + 1) * t
    period = n_keys + t
    d = np.arange(period)
    d = np.where(d < t, d, d - period)
    idx = np.clip(reach * t + d, -MAX_REL_DIST, MAX_REL_DIST) + MAX_REL_DIST
    ext = rel_table.astype(F32)[:, idx] * LOG2E
    h = rel_table.shape[0]
    rows = jnp.tile(ext, (1, n_keys))[:, :n_keys * (period - 1)].reshape(h, n_keys, period - 1)
    bias = rows[:, :, :t]
    kc = (np.arange(n_keys) // CHUNK - (reach * t // CHUNK - BAND_CHUNKS))[:, None]
    qc = (np.arange(t) // CHUNK)[None, :]
    visible = (kc >= qc) & (kc <= qc + BAND_CHUNKS)
    return jnp.where(visible[None], bias, NEG).reshape(h, reach + 1, t, t)


def _mixers(qk, vt, kaug, qaug, rel_table, lam_params, lam_init):
    b, s, _ = qk.shape
    t = min(ATTN_TILE, s)
    assert s % t == 0 and t % CHUNK == 0 and t % LANES == 0
    reach = -(-BAND_CHUNKS * CHUNK // t)
    nh = HEADS_PER_GROUP
    o_a = _mixer_call(
        _fox_kernel, qk, vt, 0, nh, [qaug, kaug],
        [pl.BlockSpec((1, s, LANES), lambda bi, qi: (bi, 0, 0)),
         pl.BlockSpec((1, s, LANES), lambda bi, qi: (bi, 0, 0))], "fox")
    bias = _band_bias(rel_table, t, reach)
    o_b = _mixer_call(
        functools.partial(_band_kernel, reach=reach), qk, vt, 1, nh, [bias],
        [_resident(bias.shape)], "band")
    q_al, k_al = _alibi_lanes(t)
    fix = _alibi_diag_fix(t)
    o_c = _mixer_call(
        functools.partial(_diff_kernel, lam_init=lam_init), qk, vt, 2, 2 * nh,
        [lam_params, q_al, k_al, fix],
        [_resident(lam_params.shape), _resident(q_al.shape), _resident(k_al.shape),
         _resident(fix.shape)], "diff")
    o_d = _mixer_call(_stick_kernel, qk, vt, 3, nh, [], [], "stick",
                      extra_scratch=[pltpu.VMEM((nh, t, t), F32)])
    return o_a, o_b, o_c, o_d


def _lambda_init(layer_idx):
    return 0.8 - 0.6 * math.exp(-0.3 * layer_idx)


def kernel(x, g_ffn1, ffn1_w_gu, ffn1_w_down, g_mix, w_in, b_f, rel_bias, diff_lambda, w_out,
           g_ffn2, ffn2_w_gu, ffn2_w_down, g_final):
    b, s, d = x.shape
    depth = g_ffn1.shape[0]
    x2 = x.reshape(b * s, d)
    for layer in range(depth):
        x2 = _ffn(x2, g_ffn1[layer], ffn1_w_gu[layer], ffn1_w_down[layer])
        qk, vt, kaug, qaug = _inproj(x2.reshape(b, s, d), g_mix[layer], w_in[layer], b_f[layer])
        mixed = _mixers(qk, vt, kaug, qaug, rel_bias[layer], diff_lambda[layer], _lambda_init(layer))
        x2 = _ffn(x2, g_ffn2[layer], ffn2_w_gu[layer], ffn2_w_down[layer], mixed, w_out[layer],
                  g_final if layer == depth - 1 else None)
    return x2.reshape(b, s, d)
```

```python
import functools
import math

import numpy as np
import jax
import jax.numpy as jnp
from jax import lax
from jax.experimental import pallas as pl
from jax.experimental.pallas import tpu as pltpu

F32 = jnp.float32
BF16 = jnp.bfloat16

LANES = 128
HEAD_DIM = 64
HEADS_PER_GROUP = 4
GROUP_WIDTH = HEADS_PER_GROUP * HEAD_DIM
N_GROUPS = 4
CHUNK = 64
BAND_CHUNKS = 8
MAX_REL_DIST = 256
DIFF_QK_DIM = HEAD_DIM // 2
FFN_RES = 0.5
RMS_EPS = 1e-6
NEG = -1e30
LOG2E = math.log2(math.e)
N_SPLIT = 3

TOKEN_TILE = 512
ATTN_TILE = 256
MXU_DIM = 256
FFN_CHUNK = 6 * MXU_DIM
PREFIX_BLOCK = MXU_DIM
VMEM_LIMIT = 56 * 1024 * 1024

_NT = (((1,), (1,)), ((), ()))


def _rmsnorm(x, g):
    y = x * lax.rsqrt(jnp.mean(x * x, axis=-1, keepdims=True) + RMS_EPS)
    return y * g


def _log_sigmoid(x):
    return jnp.minimum(x, 0.0) - jnp.log1p(jnp.exp(-jnp.abs(x)))


def _split3(x):
    h1 = x.astype(BF16)
    r1 = x - h1.astype(F32)
    h2 = r1.astype(BF16)
    h3 = (r1 - h2.astype(F32)).astype(BF16)
    return h1, h2, h3


def _split3_const(c):
    parts = []
    for _ in range(N_SPLIT):
        p = float(np.asarray(c, np.float32).astype(jnp.bfloat16).astype(np.float32))
        parts.append(p)
        c = c - p
    return parts


def _params(sem):
    return pltpu.CompilerParams(dimension_semantics=sem, vmem_limit_bytes=VMEM_LIMIT)


def _resident(shape):
    zeros = (0,) * len(shape)
    return pl.BlockSpec(shape, lambda *_: zeros, pipeline_mode=pl.Buffered(1))


def _ffn_kernel(*refs, d_ff, chunk, n_mixed, final_norm):
    x_ref, refs = refs[0], refs[1:]
    x = x_ref[...]
    if n_mixed:
        wout_ref = refs[n_mixed]
        width = wout_ref.shape[0] // n_mixed
        for grp, m_ref in enumerate(refs[:n_mixed]):
            x = x + jnp.dot(m_ref[...], wout_ref[grp * width:(grp + 1) * width, :],
                            preferred_element_type=F32)
        refs = refs[n_mixed + 1:]
    g_ref, wgu_ref, wd_ref = refs[:3]
    o_ref = refs[-1]
    h = _rmsnorm(x, g_ref[...]).astype(BF16)
    acc = jnp.zeros(x.shape, F32)
    for c0 in range(0, d_ff, chunk):
        c1 = min(c0 + chunk, d_ff)
        gate = jnp.dot(h, wgu_ref[:, c0:c1], preferred_element_type=F32)
        up = jnp.dot(h, wgu_ref[:, d_ff + c0:d_ff + c1], preferred_element_type=F32)
        act = gate * jax.nn.sigmoid(gate) * up
        acc = acc + jnp.dot(act.astype(BF16), wd_ref[c0:c1, :], preferred_element_type=F32)
    y = x + FFN_RES * acc
    o_ref[...] = _rmsnorm(y, refs[3][...]) if final_norm else y


def _ffn(x2, g, w_gu, w_down, mixed=(), w_out=None, g_final=None):
    n, d = x2.shape
    d_ff = w_down.shape[0]
    chunk = FFN_CHUNK
    tm = min(TOKEN_TILE, n)
    row_spec = pl.BlockSpec((tm, d), lambda i: (i, 0))
    args, specs = [x2], [row_spec]
    if mixed:
        for m in mixed:
            args.append(m.reshape(n, m.shape[-1]))
            specs.append(pl.BlockSpec((tm, m.shape[-1]), lambda i: (i, 0)))
        args.append(w_out.astype(BF16))
        specs.append(_resident(w_out.shape))
    args += [g.reshape(1, d), w_gu.astype(BF16), w_down.astype(BF16)]
    specs += [_resident((1, d)), _resident((d, 2 * d_ff)), _resident((d_ff, d))]
    if g_final is not None:
        args.append(g_final.reshape(1, d))
        specs.append(_resident((1, d)))
    return pl.pallas_call(
        functools.partial(_ffn_kernel, d_ff=d_ff, chunk=chunk, n_mixed=len(mixed),
                          final_norm=g_final is not None),
        grid=(n // tm,),
        in_specs=specs,
        out_specs=row_spec,
        out_shape=jax.ShapeDtypeStruct((n, d), F32),
        compiler_params=_params(("arbitrary",)),
        name="ffn",
    )(*args)


def _inproj_kernel(x_ref, g_ref, wqk_ref, cs_ref, wvt_ref, wf_ref, bf_ref,
                   qk_ref, vt_ref, kaug_ref, qaug_ref, carry_ref, *, tm, blk):
    @pl.when(pl.program_id(1) == 0)
    def _():
        carry_ref[...] = jnp.zeros_like(carry_ref)

    h = _rmsnorm(x_ref[0], g_ref[...]).astype(BF16)
    qk = jnp.dot(h, wqk_ref[...], preferred_element_type=F32)
    qk_ref[0] = (qk * cs_ref[...]).astype(BF16)
    vt_ref[0] = lax.dot_general(wvt_ref[...], h, _NT, preferred_element_type=F32).astype(BF16)

    ri = lax.broadcasted_iota(jnp.int32, (blk, blk), 0)
    ci = lax.broadcasted_iota(jnp.int32, (blk, blk), 1)
    tril = jnp.where(ci <= ri, 1.0, 0.0).astype(BF16)
    ls = _log_sigmoid(jnp.dot(h, wf_ref[...], preferred_element_type=F32) + bf_ref[...])
    lane = lax.broadcasted_iota(jnp.int32, (1, LANES), 1)
    used = lane < 2 * N_SPLIT * HEADS_PER_GROUP
    slot = lane % (2 * N_SPLIT)
    carry = carry_ref[...]
    for r0 in range(0, tm, blk):
        cum = carry
        for part in _split3(ls[r0:r0 + blk]):
            cum = cum + jnp.dot(tril, part, preferred_element_type=F32)
        carry = cum[blk - 1:blk, :]
        kaug = jnp.where(slot >= N_SPLIT, 1.0, 0.0)
        qaug = jnp.where(slot < N_SPLIT, -1.0, 0.0)
        for i, part in enumerate(_split3(cum * LOG2E)):
            kaug = jnp.where(slot == i, part.astype(F32), kaug)
            qaug = jnp.where(slot == N_SPLIT + i, part.astype(F32), qaug)
        kaug_ref[0, r0:r0 + blk, :] = jnp.where(used, kaug, 0.0).astype(BF16)
        qaug_ref[0, r0:r0 + blk, :] = jnp.where(used, qaug, 0.0).astype(BF16)
    carry_ref[...] = carry


def _inproj(x3, g, w_in, b_f):
    b, s, d = x3.shape
    gw = GROUP_WIDTH
    nf = HEADS_PER_GROUP
    starts = (0, 3 * gw + nf, 6 * gw + nf, 9 * gw + nf)
    w_qk = jnp.concatenate([w_in[:, st:st + 2 * gw] for st in starts], axis=1).astype(BF16)
    w_vt = jnp.concatenate([w_in[:, st + 2 * gw:st + 3 * gw] for st in starts], axis=1).T.astype(BF16)
    rep = 2 * N_SPLIT
    w_f = jnp.pad(jnp.repeat(w_in[:, 3 * gw:3 * gw + nf], rep, axis=1),
                  ((0, 0), (0, LANES - rep * nf))).astype(BF16)
    b_col = jnp.pad(jnp.repeat(b_f, rep), (0, LANES - rep * nf)).reshape(1, LANES)
    width = w_qk.shape[1]
    cs = np.ones((1, width), np.float32)
    for grp in range(N_GROUPS):
        scale = DIFF_QK_DIM ** -0.5 if grp == 2 else HEAD_DIM ** -0.5
        cs[0, 2 * grp * gw:(2 * grp + 1) * gw] = scale * LOG2E
    tm = min(TOKEN_TILE, s)
    blk = min(PREFIX_BLOCK, tm)
    return pl.pallas_call(
        functools.partial(_inproj_kernel, tm=tm, blk=blk),
        grid=(b, s // tm),
        in_specs=[
            pl.BlockSpec((1, tm, d), lambda bi, j: (bi, j, 0)),
            _resident((1, d)),
            _resident((d, width)),
            _resident((1, width)),
            _resident((N_GROUPS * gw, d)),
            _resident((d, LANES)),
            _resident((1, LANES)),
        ],
        out_specs=[
            pl.BlockSpec((1, tm, width), lambda bi, j: (bi, j, 0)),
            pl.BlockSpec((1, N_GROUPS * gw, tm), lambda bi, j: (bi, 0, j)),
            pl.BlockSpec((1, tm, LANES), lambda bi, j: (bi, j, 0)),
            pl.BlockSpec((1, tm, LANES), lambda bi, j: (bi, j, 0)),
        ],
        out_shape=[
            jax.ShapeDtypeStruct((b, s, width), BF16),
            jax.ShapeDtypeStruct((b, N_GROUPS * gw, s), BF16),
            jax.ShapeDtypeStruct((b, s, LANES), BF16),
            jax.ShapeDtypeStruct((b, s, LANES), BF16),
        ],
        scratch_shapes=[pltpu.VMEM((1, LANES), F32)],
        compiler_params=_params(("arbitrary", "arbitrary")),
        name="inproj",
    )(x3, g.reshape(1, d), w_qk, jnp.asarray(cs), w_vt, w_f, b_col)


def _lane_range(lo, width):
    lane = lax.broadcasted_iota(jnp.int32, (1, LANES), 1)
    return (lane >= lo) & (lane < lo + width)


def _query_rows(ref, q_tile, t):
    return ref[0, pl.ds(pl.multiple_of(q_tile * t, t), t), :]


def _masked_queries(q, width):
    out = []
    for pair in range(2):
        q_pair = q[:, pair * LANES:(pair + 1) * LANES]
        for lo in range(0, LANES, width):
            out.append(jnp.where(_lane_range(lo, width), q_pair, jnp.zeros_like(q_pair)))
    return out


def _head_scores(k_ref, qs, t, per_head, k_extra=None):
    def scores(kb):
        k0 = pl.multiple_of(kb * t, t)
        out = []
        for c, q in enumerate(qs):
            pair = c // (2 * per_head)
            k = k_ref[0, pl.ds(k0, t), pair * LANES:(pair + 1) * LANES]
            if k_extra is not None:
                k = jnp.concatenate([k, k_extra(c, k0)], axis=1)
            out.append(lax.dot_general(k, q, _NT, preferred_element_type=F32))
        return out
    return scores


def _value_rows(vt, hh):
    ones = jnp.ones((HEAD_DIM, vt.shape[1]), vt.dtype)
    if hh == 0:
        return jnp.concatenate([vt[:HEAD_DIM], ones], axis=0)
    return jnp.concatenate([ones, vt[HEAD_DIM:]], axis=0)


def _head_products(vt_ref, t, per_head, with_ones=True):
    def products(ps, kb):
        k0 = pl.multiple_of(kb * t, t)
        out = []
        for c, p in enumerate(ps):
            h = c // per_head
            vt = vt_ref[0, (h // 2) * LANES:(h // 2 + 1) * LANES, pl.ds(k0, t)]
            if with_ones:
                vt = _value_rows(vt, h % 2)
            out.append(jnp.dot(vt, p, preferred_element_type=F32))
        return out
    return products


_MAX, _ALPHA, _OFFSET = range(3)


def _flash_forward(qi, n_q, t, n_chain, tile_ops, products, s_scr, p_scr, acc_scr, st_scr):
    chains = range(n_chain)
    kb_lo, kb_hi, scores, adjust, shift_of = tile_ops(qi)

    def land(kb, tiles, diag, adjust=adjust, shift_of=shift_of):
        slot = kb & 1
        for c in chains:
            s = adjust(c, tiles[c], kb, diag)
            s_scr[slot, c] = s
            mx = jnp.max(s, axis=0, keepdims=True)
            shift = shift_of(c, kb, diag)
            if shift is not None:
                mx = mx + shift
            m_old = st_scr[_MAX, c]
            m_new = jnp.maximum(m_old, mx)
            st_scr[_MAX, c] = m_new
            st_scr[_ALPHA, c] = jnp.exp2(m_old - m_new)
            st_scr[_OFFSET, c] = m_new if shift is None else m_new - shift

    def weigh(kb):
        slot = kb & 1
        for c in chains:
            r = st_scr[_OFFSET, c]
            for lo in range(0, t, LANES):
                cols = slice(lo, lo + LANES)
                p_scr[c, :, cols] = jnp.exp2(s_scr[slot, c, :, cols] - r[:, cols]).astype(BF16)
        return products([p_scr[c] for c in chains], kb)

    def accumulate(alpha, prod):
        for c in chains:
            acc_scr[c] = alpha[c] * acc_scr[c] + prod[c]

    def step(kb, diag_next):
        alpha = [st_scr[_ALPHA, c] for c in chains]
        nxt = scores(kb + 1)
        prod = weigh(kb)
        land(kb + 1, nxt, diag_next)
        accumulate(alpha, prod)

    def reset_stats():
        for c in chains:
            st_scr[_MAX, c] = jnp.full((1, t), NEG, F32)

    for c in chains:
        acc_scr[c] = jnp.zeros((LANES, t), F32)

    @pl.when(qi == 0)
    def _():
        reset_stats()
        land(kb_lo, scores(kb_lo), True)

    def body(kb, carry):
        step(kb, False)
        return carry

    lax.fori_loop(kb_lo, kb_hi - 1, body, 0)

    @pl.when(kb_hi > kb_lo)
    def _():
        step(kb_hi - 1, True)

    alpha = [st_scr[_ALPHA, c] for c in chains]
    prod = weigh(kb_hi)
    nxt_lo, _, nxt_scores, nxt_adjust, nxt_shift = tile_ops(jnp.minimum(qi + 1, n_q - 1))
    reset_stats()
    land(nxt_lo, nxt_scores(nxt_lo), False, nxt_adjust, nxt_shift)
    return [alpha[c] * acc_scr[c] + prod[c] for c in chains]


def _normalized(acc, hh):
    if hh == 0:
        return acc[:HEAD_DIM] * (1.0 / acc[HEAD_DIM:HEAD_DIM + 1])
    return acc[HEAD_DIM:] * (1.0 / acc[0:1])


def _store_pair(o_ref, pair, top, bottom):
    o_t = jnp.concatenate([top, bottom], axis=0)
    o_ref[0, :, pair * LANES:(pair + 1) * LANES] = o_t.T.astype(o_ref.dtype)


def _store_normalized(o_ref, acc):
    for pair in range(2):
        _store_pair(o_ref, pair, _normalized(acc[2 * pair], 0), _normalized(acc[2 * pair + 1], 1))


def _tile_iotas(t):
    key = lax.broadcasted_iota(jnp.int32, (t, t), 0)
    qry = lax.broadcasted_iota(jnp.int32, (t, t), 1)
    return key, qry


def _no_shift(c, kb, diag):
    return None


def _fox_kernel(q_ref, k_ref, vt_ref, qaug_ref, kaug_ref, o_ref, s_scr, p_scr, acc_scr, st_scr,
                *, t):
    rep = 2 * N_SPLIT

    def k_extra(h, k0):
        return kaug_ref[0, pl.ds(k0, t), :]

    def adjust(h, s, kb, diag):
        if not diag:
            return s
        key, qry = _tile_iotas(t)
        return jnp.where(key <= qry, s, NEG)

    def tile_ops(q_tile):
        q_aug = _query_rows(qaug_ref, q_tile, t)
        qs = [jnp.concatenate(
                  [q, jnp.where(_lane_range(rep * h, rep), q_aug, jnp.zeros_like(q_aug))], axis=1)
              for h, q in enumerate(_masked_queries(_query_rows(q_ref, q_tile, t), HEAD_DIM))]
        return 0, q_tile, _head_scores(k_ref, qs, t, 1, k_extra), adjust, _no_shift

    acc = _flash_forward(pl.program_id(1), pl.num_programs(1), t, HEADS_PER_GROUP, tile_ops,
                         _head_products(vt_ref, t, 1), s_scr, p_scr, acc_scr, st_scr)
    _store_normalized(o_ref, acc)


def _band_kernel(q_ref, k_ref, vt_ref, bias_ref, o_ref, s_scr, p_scr, acc_scr, st_scr,
                 *, t, reach):
    def tile_ops(q_tile):
        qs = _masked_queries(_query_rows(q_ref, q_tile, t), HEAD_DIM)

        def adjust(h, s, kb, diag):
            return s + bias_ref[h, kb - q_tile + reach]

        return (jnp.maximum(q_tile - reach, 0), q_tile, _head_scores(k_ref, qs, t, 1), adjust,
                _no_shift)

    acc = _flash_forward(pl.program_id(1), pl.num_programs(1), t, HEADS_PER_GROUP, tile_ops,
                         _head_products(vt_ref, t, 1), s_scr, p_scr, acc_scr, st_scr)
    _store_normalized(o_ref, acc)


def _alibi_lanes(t):
    q_aug = np.zeros((HEADS_PER_GROUP, t, LANES), np.float32)
    k_aug = np.zeros((HEADS_PER_GROUP, t, LANES), np.float32)
    off = np.arange(t, dtype=np.float32)
    for h in range(HEADS_PER_GROUP):
        parts = _split3_const(_alibi_slope(h) * LOG2E)
        for i, c in enumerate(parts):
            q_aug[h, :, i] = off
            k_aug[h, :, i] = -c
            q_aug[h, :, N_SPLIT + i] = c
            k_aug[h, :, N_SPLIT + i] = off
    return jnp.asarray(q_aug, BF16), jnp.asarray(k_aug, BF16)


def _alibi_slope(h):
    return 2.0 ** (-8.0 * (h + 1) / HEADS_PER_GROUP)


def _alibi_diag_fix(t):
    key = np.arange(t)[:, None]
    qry = np.arange(t)[None, :]
    dist = np.minimum(qry - key, 0).astype(np.float32)
    fix = np.stack([dist * np.float32(2.0 * _alibi_slope(h) * LOG2E) for h in range(HEADS_PER_GROUP)])
    return jnp.asarray(np.where((key // CHUNK <= qry // CHUNK)[None], fix, np.float32(NEG)))


def _diff_kernel(q_ref, k_ref, vt_ref, lam_ref, qal_ref, kal_ref, fix_ref, o_ref,
                 s_scr, p_scr, acc_scr, st_scr, *, t, lam_init):
    qi = pl.program_id(1)
    lp = lam_ref[...]
    lam = (jnp.exp(jnp.sum(lp[0:1] * lp[1:2], axis=-1, keepdims=True))
           - jnp.exp(jnp.sum(lp[2:3] * lp[3:4], axis=-1, keepdims=True)) + lam_init)
    def k_extra(c, k0):
        return kal_ref[c // 2]

    def adjust(c, s, kb, diag):
        return s + fix_ref[c // 2] if diag else s

    def tile_ops(q_tile):
        qs = [jnp.concatenate([q, qal_ref[c // 2]], axis=1)
              for c, q in enumerate(_masked_queries(_query_rows(q_ref, q_tile, t), DIFF_QK_DIM))]

        def shift_of(c, kb, diag):
            if diag:
                return None
            return (-_alibi_slope(c // 2) * LOG2E * t) * (q_tile - kb).astype(F32)

        return 0, q_tile, _head_scores(k_ref, qs, t, 2, k_extra), adjust, shift_of

    n_chain = 2 * HEADS_PER_GROUP
    acc = _flash_forward(qi, pl.num_programs(1), t, n_chain, tile_ops,
                         _head_products(vt_ref, t, 2), s_scr, p_scr, acc_scr, st_scr)
    for pair in range(2):
        halves = []
        for hh in range(2):
            h = 2 * pair + hh
            o_h = _normalized(acc[2 * h], hh) - lam * _normalized(acc[2 * h + 1], hh)
            ms = jnp.mean(o_h * o_h, axis=0, keepdims=True)
            halves.append(o_h * lax.rsqrt(ms + RMS_EPS) * (1.0 - lam_init))
        _store_pair(o_ref, pair, halves[0], halves[1])


def _softplus2(z):
    neg_abs = pltpu.bitcast(pltpu.bitcast(z, jnp.uint32) | jnp.uint32(0x80000000), F32)
    return jnp.maximum(z, 0.0) + jnp.log(1.0 + jnp.exp2(neg_abs)) * LOG2E


def _stick_kernel(q_ref, k_ref, vt_ref, o_ref, z_scr, w_scr, acc_scr, st_scr, wi_scr, *, t):
    qi = pl.program_id(1)
    chains = range(HEADS_PER_GROUP)
    products = _head_products(vt_ref, t, 1, with_ones=False)
    tail_scr = st_scr.at[0]

    scores = _head_scores(k_ref, _masked_queries(_query_rows(q_ref, qi, t), HEAD_DIM), t, 1)

    def land(j, tiles, diag):
        slot = j & 1
        key, qry = _tile_iotas(t)
        minus_suffix = jnp.where(qry >= key, -1.0, 0.0).astype(BF16)
        within = []
        for c in chains:
            z = tiles[c]
            rest = _softplus2(z)
            if diag:
                rest = jnp.where(key < qry, rest, 0.0)
                z = jnp.where(key < qry, z, NEG)
            z_scr[slot, c] = z
            hi = rest.astype(BF16)
            lo = (rest - hi.astype(F32)).astype(BF16)
            within.append(jnp.dot(minus_suffix, hi, preferred_element_type=F32)
                          + jnp.dot(minus_suffix, lo, preferred_element_type=F32))
        return within

    def weigh(j):
        slot = j & 1
        for c in chains:
            tail = tail_scr[c]
            for lo in range(0, t, LANES):
                cols = slice(lo, lo + LANES)
                logw = z_scr[slot, c, :, cols] + wi_scr[c, :, cols] + tail[:, cols]
                w_scr[c, :, cols] = jnp.exp2(logw).astype(BF16)
            tail_scr[c] = tail + wi_scr[c, 0:1, :]
        return products([w_scr[c] for c in chains], qi - j)

    def collect(prod, within):
        for c in chains:
            acc_scr[c] = acc_scr[c] + prod[c]
            wi_scr[c] = within[c]

    def body(j, carry):
        nxt = scores(qi - j - 1)
        prod = weigh(j)
        collect(prod, land(j + 1, nxt, False))
        return carry

    for c, wi in enumerate(land(0, scores(qi), True)):
        tail_scr[c] = jnp.zeros((1, t), F32)
        acc_scr[c] = jnp.zeros((LANES, t), F32)
        wi_scr[c] = wi
    lax.fori_loop(0, qi, body, 0)
    prod = weigh(qi)
    acc = [acc_scr[c] + prod[c] for c in chains]
    for pair in range(2):
        _store_pair(o_ref, pair, acc[2 * pair][:HEAD_DIM], acc[2 * pair + 1][HEAD_DIM:])


def _mixer_call(body, qk, vt, group, n_chain, extra_inputs, extra_specs, name, extra_scratch=()):
    b, s, _ = qk.shape
    t = min(ATTN_TILE, s)
    gw = GROUP_WIDTH
    return pl.pallas_call(
        functools.partial(body, t=t),
        grid=(b, s // t),
        in_specs=[
            pl.BlockSpec((1, s, gw), lambda bi, qi: (bi, 0, 2 * group)),
            pl.BlockSpec((1, s, gw), lambda bi, qi: (bi, 0, 2 * group + 1)),
            pl.BlockSpec((1, gw, s), lambda bi, qi: (bi, group, 0)),
        ] + extra_specs,
        out_specs=pl.BlockSpec((1, t, gw), lambda bi, qi: (bi, qi, 0)),
        out_shape=jax.ShapeDtypeStruct((b, s, gw), BF16),
        scratch_shapes=[pltpu.VMEM((2, n_chain, t, t), F32),
                        pltpu.VMEM((n_chain, t, t), BF16),
                        pltpu.VMEM((n_chain, LANES, t), F32),
                        pltpu.VMEM((3, n_chain, 1, t), F32)]
                       + list(extra_scratch),
        compiler_params=_params(("arbitrary", "arbitrary")),
        name=name,
    )(qk, qk, vt, *extra_inputs)


def _band_bias(rel_table, t, reach):
    n_keys = (reach + 1) * t
    period = n_keys + t
    d = np.arange(period)
    d = np.where(d < t, d, d - period)
    idx = np.clip(reach * t + d, -MAX_REL_DIST, MAX_REL_DIST) + MAX_REL_DIST
    ext = rel_table.astype(F32)[:, idx] * LOG2E
    h = rel_table.shape[0]
    rows = jnp.tile(ext, (1, n_keys))[:, :n_keys * (period - 1)].reshape(h, n_keys, period - 1)
    bias = rows[:, :, :t]
    kc = (np.arange(n_keys) // CHUNK - (reach * t // CHUNK - BAND_CHUNKS))[:, None]
    qc = (np.arange(t) // CHUNK)[None, :]
    visible = (kc >= qc) & (kc <= qc + BAND_CHUNKS)
    return jnp.where(visible[None], bias, NEG).reshape(h, reach + 1, t, t)


def _mixers(qk, vt, kaug, qaug, rel_table, lam_params, lam_init):
    b, s, _ = qk.shape
    t = min(ATTN_TILE, s)
    assert s % t == 0 and t % CHUNK == 0 and t % LANES == 0
    reach = -(-BAND_CHUNKS * CHUNK // t)
    nh = HEADS_PER_GROUP
    o_a = _mixer_call(
        _fox_kernel, qk, vt, 0, nh, [qaug, kaug],
        [pl.BlockSpec((1, s, LANES), lambda bi, qi: (bi, 0, 0)),
         pl.BlockSpec((1, s, LANES), lambda bi, qi: (bi, 0, 0))], "fox")
    bias = _band_bias(rel_table, t, reach)
    o_b = _mixer_call(
        functools.partial(_band_kernel, reach=reach), qk, vt, 1, nh, [bias],
        [_resident(bias.shape)], "band")
    q_al, k_al = _alibi_lanes(t)
    fix = _alibi_diag_fix(t)
    o_c = _mixer_call(
        functools.partial(_diff_kernel, lam_init=lam_init), qk, vt, 2, 2 * nh,
        [lam_params, q_al, k_al, fix],
        [_resident(lam_params.shape), _resident(q_al.shape), _resident(k_al.shape),
         _resident(fix.shape)], "diff")
    o_d = _mixer_call(_stick_kernel, qk, vt, 3, nh, [], [], "stick",
                      extra_scratch=[pltpu.VMEM((nh, t, t), F32)])
    return o_a, o_b, o_c, o_d


def _lambda_init(layer_idx):
    return 0.8 - 0.6 * math.exp(-0.3 * layer_idx)


def kernel(x, g_ffn1, ffn1_w_gu, ffn1_w_down, g_mix, w_in, b_f, rel_bias, diff_lambda, w_out,
           g_ffn2, ffn2_w_gu, ffn2_w_down, g_final):
    b, s, d = x.shape
    depth = g_ffn1.shape[0]
    x2 = x.reshape(b * s, d)
    for layer in range(depth):
        x2 = _ffn(x2, g_ffn1[layer], ffn1_w_gu[layer], ffn1_w_down[layer])
        qk, vt, kaug, qaug = _inproj(x2.reshape(b, s, d), g_mix[layer], w_in[layer], b_f[layer])
        mixed = _mixers(qk, vt, kaug, qaug, rel_bias[layer], diff_lambda[layer], _lambda_init(layer))
        x2 = _ffn(x2, g_ffn2[layer], ffn2_w_gu[layer], ffn2_w_down[layer], mixed, w_out[layer],
                  g_final if layer == depth - 1 else None)
    return x2.reshape(b, s, d)
```

```python
import functools
import math

import numpy as np
import jax
import jax.numpy as jnp
from jax import lax
from jax.experimental import pallas as pl
from jax.experimental.pallas import tpu as pltpu

F32 = jnp.float32
BF16 = jnp.bfloat16

LANES = 128
HEAD_DIM = 64
HEADS_PER_GROUP = 4
GROUP_WIDTH = HEADS_PER_GROUP * HEAD_DIM
N_GROUPS = 4
CHUNK = 64
BAND_CHUNKS = 8
MAX_REL_DIST = 256
DIFF_QK_DIM = HEAD_DIM // 2
FFN_RES = 0.5
RMS_EPS = 1e-6
NEG = -1e30
LOG2E = math.log2(math.e)
N_SPLIT = 3
ONES_ROWS = 16

TOKEN_TILE = 512
ATTN_TILE = 256
CHAIN_GROUP = 2
MXU_DIM = 256
FFN_CHUNK = 6 * MXU_DIM
PREFIX_BLOCK = MXU_DIM
VMEM_LIMIT = 56 * 1024 * 1024

_NT = (((1,), (1,)), ((), ()))


def _rmsnorm(x, g):
    y = x * lax.rsqrt(jnp.mean(x * x, axis=-1, keepdims=True) + RMS_EPS)
    return y * g


def _log_sigmoid(x):
    return jnp.minimum(x, 0.0) - jnp.log1p(jnp.exp(-jnp.abs(x)))


def _split3(x):
    h1 = x.astype(BF16)
    r1 = x - h1.astype(F32)
    h2 = r1.astype(BF16)
    h3 = (r1 - h2.astype(F32)).astype(BF16)
    return h1, h2, h3


def _split3_const(c):
    parts = []
    for _ in range(N_SPLIT):
        p = float(np.asarray(c, np.float32).astype(jnp.bfloat16).astype(np.float32))
        parts.append(p)
        c = c - p
    return parts


def _params(sem):
    return pltpu.CompilerParams(dimension_semantics=sem, vmem_limit_bytes=VMEM_LIMIT)


def _resident(shape):
    zeros = (0,) * len(shape)
    return pl.BlockSpec(shape, lambda *_: zeros, pipeline_mode=pl.Buffered(1))


def _ffn_kernel(*refs, d_ff, chunk, n_mixed, final_norm):
    x_ref, refs = refs[0], refs[1:]
    x = x_ref[...]
    if n_mixed:
        wout_ref = refs[n_mixed]
        width = wout_ref.shape[0] // n_mixed
        for grp, m_ref in enumerate(refs[:n_mixed]):
            x = x + jnp.dot(m_ref[...], wout_ref[grp * width:(grp + 1) * width, :],
                            preferred_element_type=F32)
        refs = refs[n_mixed + 1:]
    g_ref, wgu_ref, wd_ref = refs[:3]
    o_ref = refs[-1]
    h = _rmsnorm(x, g_ref[...]).astype(BF16)
    acc = jnp.zeros(x.shape, F32)
    for c0 in range(0, d_ff, chunk):
        c1 = min(c0 + chunk, d_ff)
        gate = jnp.dot(h, wgu_ref[:, c0:c1], preferred_element_type=F32)
        up = jnp.dot(h, wgu_ref[:, d_ff + c0:d_ff + c1], preferred_element_type=F32)
        act = gate * jax.nn.sigmoid(gate) * up
        acc = acc + jnp.dot(act.astype(BF16), wd_ref[c0:c1, :], preferred_element_type=F32)
    y = x + FFN_RES * acc
    o_ref[...] = _rmsnorm(y, refs[3][...]) if final_norm else y


def _ffn(x2, g, w_gu, w_down, mixed=(), w_out=None, g_final=None):
    n, d = x2.shape
    d_ff = w_down.shape[0]
    chunk = FFN_CHUNK
    tm = min(TOKEN_TILE, n)
    row_spec = pl.BlockSpec((tm, d), lambda i: (i, 0))
    args, specs = [x2], [row_spec]
    if mixed:
        for m in mixed:
            args.append(m.reshape(n, m.shape[-1]))
            specs.append(pl.BlockSpec((tm, m.shape[-1]), lambda i: (i, 0)))
        args.append(w_out.astype(BF16))
        specs.append(_resident(w_out.shape))
    args += [g.reshape(1, d), w_gu.astype(BF16), w_down.astype(BF16)]
    specs += [_resident((1, d)), _resident((d, 2 * d_ff)), _resident((d_ff, d))]
    if g_final is not None:
        args.append(g_final.reshape(1, d))
        specs.append(_resident((1, d)))
    return pl.pallas_call(
        functools.partial(_ffn_kernel, d_ff=d_ff, chunk=chunk, n_mixed=len(mixed),
                          final_norm=g_final is not None),
        grid=(n // tm,),
        in_specs=specs,
        out_specs=row_spec,
        out_shape=jax.ShapeDtypeStruct((n, d), F32),
        compiler_params=_params(("arbitrary",)),
        name="ffn",
    )(*args)


def _inproj_kernel(x_ref, g_ref, wqk_ref, cs_ref, wvt_ref, wf_ref, bf_ref,
                   qk_ref, vt_ref, kaug_ref, qaug_ref, carry_ref, *, tm, blk):
    @pl.when(pl.program_id(1) == 0)
    def _():
        carry_ref[...] = jnp.zeros_like(carry_ref)

    h = _rmsnorm(x_ref[0], g_ref[...]).astype(BF16)
    qk = jnp.dot(h, wqk_ref[...], preferred_element_type=F32)
    qk_ref[0] = (qk * cs_ref[...]).astype(BF16)
    vt_ref[0] = lax.dot_general(wvt_ref[...], h, _NT, preferred_element_type=F32).astype(BF16)

    ri = lax.broadcasted_iota(jnp.int32, (blk, blk), 0)
    ci = lax.broadcasted_iota(jnp.int32, (blk, blk), 1)
    tril = jnp.where(ci <= ri, 1.0, 0.0).astype(BF16)
    ls = _log_sigmoid(jnp.dot(h, wf_ref[...], preferred_element_type=F32) + bf_ref[...])
    lane = lax.broadcasted_iota(jnp.int32, (1, LANES), 1)
    used = lane < 2 * N_SPLIT * HEADS_PER_GROUP
    slot = lane % (2 * N_SPLIT)
    carry = carry_ref[...]
    for r0 in range(0, tm, blk):
        cum = carry
        for part in _split3(ls[r0:r0 + blk]):
            cum = cum + jnp.dot(tril, part, preferred_element_type=F32)
        carry = cum[blk - 1:blk, :]
        kaug = jnp.where(slot >= N_SPLIT, 1.0, 0.0)
        qaug = jnp.where(slot < N_SPLIT, -1.0, 0.0)
        for i, part in enumerate(_split3(cum * LOG2E)):
            kaug = jnp.where(slot == i, part.astype(F32), kaug)
            qaug = jnp.where(slot == N_SPLIT + i, part.astype(F32), qaug)
        kaug_ref[0, r0:r0 + blk, :] = jnp.where(used, kaug, 0.0).astype(BF16)
        qaug_ref[0, r0:r0 + blk, :] = jnp.where(used, qaug, 0.0).astype(BF16)
    carry_ref[...] = carry


def _inproj(x3, g, w_in, b_f):
    b, s, d = x3.shape
    gw = GROUP_WIDTH
    nf = HEADS_PER_GROUP
    starts = (0, 3 * gw + nf, 6 * gw + nf, 9 * gw + nf)
    w_qk = jnp.concatenate([w_in[:, st:st + 2 * gw] for st in starts], axis=1).astype(BF16)
    w_vt = jnp.concatenate([w_in[:, st + 2 * gw:st + 3 * gw] for st in starts], axis=1).T.astype(BF16)
    rep = 2 * N_SPLIT
    w_f = jnp.pad(jnp.repeat(w_in[:, 3 * gw:3 * gw + nf], rep, axis=1),
                  ((0, 0), (0, LANES - rep * nf))).astype(BF16)
    b_col = jnp.pad(jnp.repeat(b_f, rep), (0, LANES - rep * nf)).reshape(1, LANES)
    width = w_qk.shape[1]
    cs = np.ones((1, width), np.float32)
    for grp in range(N_GROUPS):
        scale = DIFF_QK_DIM ** -0.5 if grp == 2 else HEAD_DIM ** -0.5
        cs[0, 2 * grp * gw:(2 * grp + 1) * gw] = scale * LOG2E
    tm = min(TOKEN_TILE, s)
    blk = min(PREFIX_BLOCK, tm)
    return pl.pallas_call(
        functools.partial(_inproj_kernel, tm=tm, blk=blk),
        grid=(b, s // tm),
        in_specs=[
            pl.BlockSpec((1, tm, d), lambda bi, j: (bi, j, 0)),
            _resident((1, d)),
            _resident((d, width)),
            _resident((1, width)),
            _resident((N_GROUPS * gw, d)),
            _resident((d, LANES)),
            _resident((1, LANES)),
        ],
        out_specs=[
            pl.BlockSpec((1, tm, width), lambda bi, j: (bi, j, 0)),
            pl.BlockSpec((1, N_GROUPS * gw, tm), lambda bi, j: (bi, 0, j)),
            pl.BlockSpec((1, tm, LANES), lambda bi, j: (bi, j, 0)),
            pl.BlockSpec((1, tm, LANES), lambda bi, j: (bi, j, 0)),
        ],
        out_shape=[
            jax.ShapeDtypeStruct((b, s, width), BF16),
            jax.ShapeDtypeStruct((b, N_GROUPS * gw, s), BF16),
            jax.ShapeDtypeStruct((b, s, LANES), BF16),
            jax.ShapeDtypeStruct((b, s, LANES), BF16),
        ],
        scratch_shapes=[pltpu.VMEM((1, LANES), F32)],
        compiler_params=_params(("arbitrary", "arbitrary")),
        name="inproj",
    )(x3, g.reshape(1, d), w_qk, jnp.asarray(cs), w_vt, w_f, b_col)


def _lane_range(lo, width):
    lane = lax.broadcasted_iota(jnp.int32, (1, LANES), 1)
    return (lane >= lo) & (lane < lo + width)


def _query_rows(ref, q_tile, t):
    return ref[0, pl.ds(pl.multiple_of(q_tile * t, t), t), :]


def _masked_queries(q, width):
    out = []
    for pair in range(2):
        q_pair = q[:, pair * LANES:(pair + 1) * LANES]
        for lo in range(0, LANES, width):
            out.append(jnp.where(_lane_range(lo, width), q_pair, jnp.zeros_like(q_pair)))
    return out


def _head_scores(k_ref, qs, t, per_head, k_extra=None):
    def scores(kb, cs):
        k0 = pl.multiple_of(kb * t, t)
        out = []
        for c in cs:
            pair = c // (2 * per_head)
            k = k_ref[0, pl.ds(k0, t), pair * LANES:(pair + 1) * LANES]
            if k_extra is not None:
                k = jnp.concatenate([k, k_extra(c, k0)], axis=1)
            out.append(lax.dot_general(k, qs[c], _NT, preferred_element_type=F32))
        return out
    return scores


def _head_products(vt_ref, t, per_head, with_ones=True):
    def products(ps, kb, cs):
        k0 = pl.multiple_of(kb * t, t)
        out = []
        for c, p in zip(cs, ps):
            h = c // per_head
            vt = vt_ref[0, h * HEAD_DIM:(h + 1) * HEAD_DIM, pl.ds(k0, t)]
            if with_ones:
                vt = jnp.concatenate([vt, jnp.ones((ONES_ROWS, t), vt.dtype)], axis=0)
            out.append(jnp.dot(vt, p, preferred_element_type=F32))
        return out
    return products


_MAX, _ALPHA, _OFFSET = range(3)


def _flash_forward(qi, n_q, t, n_chain, group, tile_ops, products,
                   s_scr, p_scr, acc_scr, st_scr):
    chains = range(n_chain)
    kb_lo, kb_hi, scores, adjust, shift_of = tile_ops(qi)

    def land(kb, tiles, diag, cs=chains, adjust=adjust, shift_of=shift_of):
        slot = kb & 1
        for c, tile in zip(cs, tiles):
            s = adjust(c, tile, kb, diag)
            s_scr[slot, c] = s
            mx = jnp.max(s, axis=0, keepdims=True)
            shift = shift_of(c, kb, diag)
            if shift is not None:
                mx = mx + shift
            m_old = st_scr[_MAX, c]
            m_new = jnp.maximum(m_old, mx)
            st_scr[_MAX, c] = m_new
            st_scr[_ALPHA, c] = jnp.exp2(m_old - m_new)
            st_scr[_OFFSET, c] = m_new if shift is None else m_new - shift

    def weigh(kb, cs=chains):
        slot = kb & 1
        for c in cs:
            r = st_scr[_OFFSET, c]
            for lo in range(0, t, LANES):
                cols = slice(lo, lo + LANES)
                p_scr[c, :, cols] = jnp.exp2(s_scr[slot, c, :, cols] - r[:, cols]).astype(BF16)
        return products([p_scr[c] for c in cs], kb, cs)

    def step(kb, diag_next):
        for g0 in range(0, n_chain, group):
            cs = range(g0, min(g0 + group, n_chain))
            alpha = [st_scr[_ALPHA, c] for c in cs]
            nxt = scores(kb + 1, cs)
            prod = weigh(kb, cs)
            land(kb + 1, nxt, diag_next, cs)
            for c, a, pr in zip(cs, alpha, prod):
                acc_scr[c] = a * acc_scr[c] + pr

    def reset_stats():
        for c in chains:
            st_scr[_MAX, c] = jnp.full((1, t), NEG, F32)

    for c in chains:
        acc_scr[c] = jnp.zeros(acc_scr.shape[1:], F32)

    @pl.when(qi == 0)
    def _():
        reset_stats()
        land(kb_lo, scores(kb_lo, chains), True)

    def body(kb, carry):
        step(kb, False)
        return carry

    lax.fori_loop(kb_lo, kb_hi - 1, body, 0)

    @pl.when(kb_hi > kb_lo)
    def _():
        step(kb_hi - 1, True)

    alpha = [st_scr[_ALPHA, c] for c in chains]
    prod = weigh(kb_hi)
    nxt_lo, _, nxt_scores, nxt_adjust, nxt_shift = tile_ops(jnp.minimum(qi + 1, n_q - 1))
    reset_stats()
    land(nxt_lo, nxt_scores(nxt_lo, chains), False, chains, nxt_adjust, nxt_shift)
    return [alpha[c] * acc_scr[c] + prod[c] for c in chains]


def _normalized(acc):
    return acc[:HEAD_DIM] * (1.0 / acc[HEAD_DIM:HEAD_DIM + 1])


def _store_pair(o_ref, pair, top, bottom):
    o_t = jnp.concatenate([top, bottom], axis=0)
    o_ref[0, :, pair * LANES:(pair + 1) * LANES] = o_t.T.astype(o_ref.dtype)


def _store_normalized(o_ref, acc):
    for pair in range(2):
        _store_pair(o_ref, pair, _normalized(acc[2 * pair]), _normalized(acc[2 * pair + 1]))


def _tile_iotas(t):
    key = lax.broadcasted_iota(jnp.int32, (t, t), 0)
    qry = lax.broadcasted_iota(jnp.int32, (t, t), 1)
    return key, qry


def _no_shift(c, kb, diag):
    return None


def _fox_kernel(q_ref, k_ref, vt_ref, qaug_ref, kaug_ref, o_ref, s_scr, p_scr, acc_scr, st_scr,
                *, t):
    rep = 2 * N_SPLIT

    def k_extra(h, k0):
        return kaug_ref[0, pl.ds(k0, t), :]

    def adjust(h, s, kb, diag):
        if not diag:
            return s
        key, qry = _tile_iotas(t)
        return jnp.where(key <= qry, s, NEG)

    def tile_ops(q_tile):
        q_aug = _query_rows(qaug_ref, q_tile, t)
        qs = [jnp.concatenate(
                  [q, jnp.where(_lane_range(rep * h, rep), q_aug, jnp.zeros_like(q_aug))], axis=1)
              for h, q in enumerate(_masked_queries(_query_rows(q_ref, q_tile, t), HEAD_DIM))]
        return 0, q_tile, _head_scores(k_ref, qs, t, 1, k_extra), adjust, _no_shift

    acc = _flash_forward(pl.program_id(1), pl.num_programs(1), t, HEADS_PER_GROUP, 2 * CHAIN_GROUP,
                         tile_ops, _head_products(vt_ref, t, 1), s_scr, p_scr, acc_scr, st_scr)
    _store_normalized(o_ref, acc)


def _band_kernel(q_ref, k_ref, vt_ref, bias_ref, o_ref, s_scr, p_scr, acc_scr, st_scr,
                 *, t, reach):
    def tile_ops(q_tile):
        qs = _masked_queries(_query_rows(q_ref, q_tile, t), HEAD_DIM)

        def adjust(h, s, kb, diag):
            return s + bias_ref[h, kb - q_tile + reach]

        return (jnp.maximum(q_tile - reach, 0), q_tile, _head_scores(k_ref, qs, t, 1), adjust,
                _no_shift)

    acc = _flash_forward(pl.program_id(1), pl.num_programs(1), t, HEADS_PER_GROUP, CHAIN_GROUP,
                         tile_ops, _head_products(vt_ref, t, 1), s_scr, p_scr, acc_scr, st_scr)
    _store_normalized(o_ref, acc)


def _alibi_lanes(t):
    q_aug = np.zeros((HEADS_PER_GROUP, t, LANES), np.float32)
    k_aug = np.zeros((HEADS_PER_GROUP, t, LANES), np.float32)
    off = np.arange(t, dtype=np.float32)
    for h in range(HEADS_PER_GROUP):
        parts = _split3_const(_alibi_slope(h) * LOG2E)
        for i, c in enumerate(parts):
            q_aug[h, :, i] = off
            k_aug[h, :, i] = -c
            q_aug[h, :, N_SPLIT + i] = c
            k_aug[h, :, N_SPLIT + i] = off
    return jnp.asarray(q_aug, BF16), jnp.asarray(k_aug, BF16)


def _alibi_slope(h):
    return 2.0 ** (-8.0 * (h + 1) / HEADS_PER_GROUP)


def _alibi_diag_fix(t):
    key = np.arange(t)[:, None]
    qry = np.arange(t)[None, :]
    dist = np.minimum(qry - key, 0).astype(np.float32)
    fix = np.stack([dist * np.float32(2.0 * _alibi_slope(h) * LOG2E) for h in range(HEADS_PER_GROUP)])
    return jnp.asarray(np.where((key // CHUNK <= qry // CHUNK)[None], fix, np.float32(NEG)))


def _diff_kernel(q_ref, k_ref, vt_ref, lam_ref, qal_ref, kal_ref, fix_ref, o_ref,
                 s_scr, p_scr, acc_scr, st_scr, *, t, lam_init):
    qi = pl.program_id(1)
    lp = lam_ref[...]
    lam = (jnp.exp(jnp.sum(lp[0:1] * lp[1:2], axis=-1, keepdims=True))
           - jnp.exp(jnp.sum(lp[2:3] * lp[3:4], axis=-1, keepdims=True)) + lam_init)
    def k_extra(c, k0):
        return kal_ref[c // 2]

    def adjust(c, s, kb, diag):
        return s + fix_ref[c // 2] if diag else s

    def tile_ops(q_tile):
        qs = [jnp.concatenate([q, qal_ref[c // 2]], axis=1)
              for c, q in enumerate(_masked_queries(_query_rows(q_ref, q_tile, t), DIFF_QK_DIM))]

        def shift_of(c, kb, diag):
            if diag:
                return None
            return (-_alibi_slope(c // 2) * LOG2E * t) * (q_tile - kb).astype(F32)

        return 0, q_tile, _head_scores(k_ref, qs, t, 2, k_extra), adjust, shift_of

    n_chain = 2 * HEADS_PER_GROUP
    acc = _flash_forward(qi, pl.num_programs(1), t, n_chain, 2 * CHAIN_GROUP, tile_ops,
                         _head_products(vt_ref, t, 2), s_scr, p_scr, acc_scr, st_scr)
    for pair in range(2):
        halves = []
        for hh in range(2):
            h = 2 * pair + hh
            o_h = _normalized(acc[2 * h]) - lam * _normalized(acc[2 * h + 1])
            ms = jnp.mean(o_h * o_h, axis=0, keepdims=True)
            halves.append(o_h * lax.rsqrt(ms + RMS_EPS) * (1.0 - lam_init))
        _store_pair(o_ref, pair, halves[0], halves[1])


def _softplus2(z):
    neg_abs = pltpu.bitcast(pltpu.bitcast(z, jnp.uint32) | jnp.uint32(0x80000000), F32)
    return jnp.maximum(z, 0.0) + jnp.log(1.0 + jnp.exp2(neg_abs)) * LOG2E


def _stick_kernel(q_ref, k_ref, vt_ref, o_ref, z_scr, w_scr, acc_scr, st_scr, wi_scr, *, t):
    qi = pl.program_id(1)
    chains = range(HEADS_PER_GROUP)
    products = _head_products(vt_ref, t, 1, with_ones=False)
    tail_scr = st_scr.at[0]

    scores = _head_scores(k_ref, _masked_queries(_query_rows(q_ref, qi, t), HEAD_DIM), t, 1)

    def land(j, tiles, diag, cs=chains):
        slot = j & 1
        key, qry = _tile_iotas(t)
        minus_suffix = jnp.where(qry >= key, -1.0, 0.0).astype(BF16)
        within = []
        for c, z in zip(cs, tiles):
            rest = _softplus2(z)
            if diag:
                rest = jnp.where(key < qry, rest, 0.0)
                z = jnp.where(key < qry, z, NEG)
            z_scr[slot, c] = z
            hi = rest.astype(BF16)
            lo = (rest - hi.astype(F32)).astype(BF16)
            within.append(jnp.dot(minus_suffix, hi, preferred_element_type=F32)
                          + jnp.dot(minus_suffix, lo, preferred_element_type=F32))
        return within

    def weigh(j, cs=chains):
        slot = j & 1
        for c in cs:
            tail = tail_scr[c]
            for lo in range(0, t, LANES):
                cols = slice(lo, lo + LANES)
                logw = z_scr[slot, c, :, cols] + wi_scr[c, :, cols] + tail[:, cols]
                w_scr[c, :, cols] = jnp.exp2(logw).astype(BF16)
            tail_scr[c] = tail + wi_scr[c, 0:1, :]
        return products([w_scr[c] for c in cs], qi - j, cs)

    def body(j, carry):
        for g0 in range(0, HEADS_PER_GROUP, 2 * CHAIN_GROUP):
            cs = range(g0, g0 + 2 * CHAIN_GROUP)
            nxt = scores(qi - j - 1, cs)
            prod = weigh(j, cs)
            within = land(j + 1, nxt, False, cs)
            for c, pr, wi in zip(cs, prod, within):
                acc_scr[c] = acc_scr[c] + pr
                wi_scr[c] = wi
        return carry

    for c, wi in enumerate(land(0, scores(qi, chains), True)):
        tail_scr[c] = jnp.zeros((1, t), F32)
        acc_scr[c] = jnp.zeros(acc_scr.shape[1:], F32)
        wi_scr[c] = wi
    lax.fori_loop(0, qi, body, 0)
    prod = weigh(qi)
    acc = [acc_scr[c] + prod[c] for c in chains]
    for pair in range(2):
        _store_pair(o_ref, pair, acc[2 * pair], acc[2 * pair + 1])


def _mixer_call(body, qk, vt, group, n_chain, extra_inputs, extra_specs, name,
                acc_rows=HEAD_DIM + ONES_ROWS, extra_scratch=()):
    b, s, _ = qk.shape
    t = min(ATTN_TILE, s)
    gw = GROUP_WIDTH
    return pl.pallas_call(
        functools.partial(body, t=t),
        grid=(b, s // t),
        in_specs=[
            pl.BlockSpec((1, s, gw), lambda bi, qi: (bi, 0, 2 * group)),
            pl.BlockSpec((1, s, gw), lambda bi, qi: (bi, 0, 2 * group + 1)),
            pl.BlockSpec((1, gw, s), lambda bi, qi: (bi, group, 0)),
        ] + extra_specs,
        out_specs=pl.BlockSpec((1, t, gw), lambda bi, qi: (bi, qi, 0)),
        out_shape=jax.ShapeDtypeStruct((b, s, gw), BF16),
        scratch_shapes=[pltpu.VMEM((2, n_chain, t, t), F32),
                        pltpu.VMEM((n_chain, t, t), BF16),
                        pltpu.VMEM((n_chain, acc_rows, t), F32),
                        pltpu.VMEM((3, n_chain, 1, t), F32)]
                       + list(extra_scratch),
        compiler_params=_params(("arbitrary", "arbitrary")),
        name=name,
    )(qk, qk, vt, *extra_inputs)


def _band_bias(rel_table, t, reach):
    n_keys = (reach + 1) * t
    period = n_keys + t
    d = np.arange(period)
    d = np.where(d < t, d, d - period)
    idx = np.clip(reach * t + d, -MAX_REL_DIST, MAX_REL_DIST) + MAX_REL_DIST
    ext = rel_table.astype(F32)[:, idx] * LOG2E
    h = rel_table.shape[0]
    rows = jnp.tile(ext, (1, n_keys))[:, :n_keys * (period - 1)].reshape(h, n_keys, period - 1)
    bias = rows[:, :, :t]
    kc = (np.arange(n_keys) // CHUNK - (reach * t // CHUNK - BAND_CHUNKS))[:, None]
    qc = (np.arange(t) // CHUNK)[None, :]
    visible = (kc >= qc) & (kc <= qc + BAND_CHUNKS)
    return jnp.where(visible[None], bias, NEG).reshape(h, reach + 1, t, t)


def _mixers(qk, vt, kaug, qaug, rel_table, lam_params, lam_init):
    b, s, _ = qk.shape
    t = min(ATTN_TILE, s)
    assert s % t == 0 and t % CHUNK == 0 and t % LANES == 0
    reach = -(-BAND_CHUNKS * CHUNK // t)
    nh = HEADS_PER_GROUP
    o_a = _mixer_call(
        _fox_kernel, qk, vt, 0, nh, [qaug, kaug],
        [pl.BlockSpec((1, s, LANES), lambda bi, qi: (bi, 0, 0)),
         pl.BlockSpec((1, s, LANES), lambda bi, qi: (bi, 0, 0))], "fox")
    bias = _band_bias(rel_table, t, reach)
    o_b = _mixer_call(
        functools.partial(_band_kernel, reach=reach), qk, vt, 1, nh, [bias],
        [_resident(bias.shape)], "band")
    q_al, k_al = _alibi_lanes(t)
    fix = _alibi_diag_fix(t)
    o_c = _mixer_call(
        functools.partial(_diff_kernel, lam_init=lam_init), qk, vt, 2, 2 * nh,
        [lam_params, q_al, k_al, fix],
        [_resident(lam_params.shape), _resident(q_al.shape), _resident(k_al.shape),
         _resident(fix.shape)], "diff")
    o_d = _mixer_call(_stick_kernel, qk, vt, 3, nh, [], [], "stick", acc_rows=HEAD_DIM,
                      extra_scratch=[pltpu.VMEM((nh, t, t), F32)])
    return o_a, o_b, o_c, o_d


def _lambda_init(layer_idx):
    return 0.8 - 0.6 * math.exp(-0.3 * layer_idx)


def kernel(x, g_ffn1, ffn1_w_gu, ffn1_w_down, g_mix, w_in, b_f, rel_bias, diff_lambda, w_out,
           g_ffn2, ffn2_w_gu, ffn2_w_down, g_final):
    b, s, d = x.shape
    depth = g_ffn1.shape[0]
    x2 = x.reshape(b * s, d)
    for layer in range(depth):
        x2 = _ffn(x2, g_ffn1[layer], ffn1_w_gu[layer], ffn1_w_down[layer])
        qk, vt, kaug, qaug = _inproj(x2.reshape(b, s, d), g_mix[layer], w_in[layer], b_f[layer])
        mixed = _mixers(qk, vt, kaug, qaug, rel_bias[layer], diff_lambda[layer], _lambda_init(layer))
        x2 = _ffn(x2, g_ffn2[layer], ffn2_w_gu[layer], ffn2_w_down[layer], mixed, w_out[layer],
                  g_final if layer == depth - 1 else None)
    return x2.reshape(b, s, d)
```

```python
import functools
import math

import numpy as np
import jax
import jax.numpy as jnp
from jax import lax
from jax.experimental import pallas as pl
from jax.experimental.pallas import tpu as pltpu

F32 = jnp.float32
BF16 = jnp.bfloat16

LANES = 128
HEAD_DIM = 64
HEADS_PER_GROUP = 4
GROUP_WIDTH = HEADS_PER_GROUP * HEAD_DIM
N_GROUPS = 4
CHUNK = 64
BAND_CHUNKS = 8
MAX_REL_DIST = 256
DIFF_QK_DIM = HEAD_DIM // 2
FFN_RES = 0.5
RMS_EPS = 1e-6
NEG = -1e30
LOG2E = math.log2(math.e)
N_SPLIT = 3
ONES_ROWS = 16

TOKEN_TILE = 512
ATTN_TILE = 256
CHAIN_GROUP = 2
MXU_DIM = 256
FFN_CHUNK = 6 * MXU_DIM
PREFIX_BLOCK = MXU_DIM
VMEM_LIMIT = 56 * 1024 * 1024

_NT = (((1,), (1,)), ((), ()))


def _rmsnorm(x, g):
    y = x * lax.rsqrt(jnp.mean(x * x, axis=-1, keepdims=True) + RMS_EPS)
    return y * g


def _log_sigmoid(x):
    return jnp.minimum(x, 0.0) - jnp.log1p(jnp.exp(-jnp.abs(x)))


def _split3(x):
    h1 = x.astype(BF16)
    r1 = x - h1.astype(F32)
    h2 = r1.astype(BF16)
    h3 = (r1 - h2.astype(F32)).astype(BF16)
    return h1, h2, h3


def _split3_const(c):
    parts = []
    for _ in range(N_SPLIT):
        p = float(np.asarray(c, np.float32).astype(jnp.bfloat16).astype(np.float32))
        parts.append(p)
        c = c - p
    return parts


def _params(sem):
    return pltpu.CompilerParams(dimension_semantics=sem, vmem_limit_bytes=VMEM_LIMIT)


def _resident(shape):
    zeros = (0,) * len(shape)
    return pl.BlockSpec(shape, lambda *_: zeros, pipeline_mode=pl.Buffered(1))


def _ffn_kernel(*refs, d_ff, chunk, n_mixed, final_norm):
    x_ref, refs = refs[0], refs[1:]
    x = x_ref[...]
    if n_mixed:
        wout_ref = refs[n_mixed]
        width = wout_ref.shape[0] // n_mixed
        for grp, m_ref in enumerate(refs[:n_mixed]):
            x = x + jnp.dot(m_ref[...], wout_ref[grp * width:(grp + 1) * width, :],
                            preferred_element_type=F32)
        refs = refs[n_mixed + 1:]
    g_ref, wgu_ref, wd_ref = refs[:3]
    o_ref = refs[-1]
    h = _rmsnorm(x, g_ref[...]).astype(BF16)
    acc = jnp.zeros(x.shape, F32)
    for c0 in range(0, d_ff, chunk):
        c1 = min(c0 + chunk, d_ff)
        gate = jnp.dot(h, wgu_ref[:, c0:c1], preferred_element_type=F32)
        up = jnp.dot(h, wgu_ref[:, d_ff + c0:d_ff + c1], preferred_element_type=F32)
        act = gate * jax.nn.sigmoid(gate) * up
        acc = acc + jnp.dot(act.astype(BF16), wd_ref[c0:c1, :], preferred_element_type=F32)
    y = x + FFN_RES * acc
    o_ref[...] = _rmsnorm(y, refs[3][...]) if final_norm else y


def _ffn(x2, g, w_gu, w_down, mixed=(), w_out=None, g_final=None):
    n, d = x2.shape
    d_ff = w_down.shape[0]
    chunk = FFN_CHUNK
    tm = min(TOKEN_TILE, n)
    row_spec = pl.BlockSpec((tm, d), lambda i: (i, 0))
    args, specs = [x2], [row_spec]
    if mixed:
        for m in mixed:
            args.append(m.reshape(n, m.shape[-1]))
            specs.append(pl.BlockSpec((tm, m.shape[-1]), lambda i: (i, 0)))
        args.append(w_out.astype(BF16))
        specs.append(_resident(w_out.shape))
    args += [g.reshape(1, d), w_gu.astype(BF16), w_down.astype(BF16)]
    specs += [_resident((1, d)), _resident((d, 2 * d_ff)), _resident((d_ff, d))]
    if g_final is not None:
        args.append(g_final.reshape(1, d))
        specs.append(_resident((1, d)))
    return pl.pallas_call(
        functools.partial(_ffn_kernel, d_ff=d_ff, chunk=chunk, n_mixed=len(mixed),
                          final_norm=g_final is not None),
        grid=(n // tm,),
        in_specs=specs,
        out_specs=row_spec,
        out_shape=jax.ShapeDtypeStruct((n, d), F32),
        compiler_params=_params(("arbitrary",)),
        name="ffn",
    )(*args)


def _inproj_kernel(x_ref, g_ref, wqk_ref, cs_ref, wvt_ref, wf_ref, bf_ref,
                   qk_ref, vt_ref, kaug_ref, qaug_ref, carry_ref, *, tm, blk):
    @pl.when(pl.program_id(1) == 0)
    def _():
        carry_ref[...] = jnp.zeros_like(carry_ref)

    h = _rmsnorm(x_ref[0], g_ref[...]).astype(BF16)
    qk = jnp.dot(h, wqk_ref[...], preferred_element_type=F32)
    qk_ref[0] = (qk * cs_ref[...]).astype(BF16)
    vt_ref[0] = lax.dot_general(wvt_ref[...], h, _NT, preferred_element_type=F32).astype(BF16)

    ri = lax.broadcasted_iota(jnp.int32, (blk, blk), 0)
    ci = lax.broadcasted_iota(jnp.int32, (blk, blk), 1)
    tril = jnp.where(ci <= ri, 1.0, 0.0).astype(BF16)
    ls = _log_sigmoid(jnp.dot(h, wf_ref[...], preferred_element_type=F32) + bf_ref[...])
    lane = lax.broadcasted_iota(jnp.int32, (1, LANES), 1)
    used = lane < 2 * N_SPLIT * HEADS_PER_GROUP
    slot = lane % (2 * N_SPLIT)
    carry = carry_ref[...]
    for r0 in range(0, tm, blk):
        cum = carry
        for part in _split3(ls[r0:r0 + blk]):
            cum = cum + jnp.dot(tril, part, preferred_element_type=F32)
        carry = cum[blk - 1:blk, :]
        kaug = jnp.where(slot >= N_SPLIT, 1.0, 0.0)
        qaug = jnp.where(slot < N_SPLIT, -1.0, 0.0)
        for i, part in enumerate(_split3(cum * LOG2E)):
            kaug = jnp.where(slot == i, part.astype(F32), kaug)
            qaug = jnp.where(slot == N_SPLIT + i, part.astype(F32), qaug)
        kaug_ref[0, r0:r0 + blk, :] = jnp.where(used, kaug, 0.0).astype(BF16)
        qaug_ref[0, r0:r0 + blk, :] = jnp.where(used, qaug, 0.0).astype(BF16)
    carry_ref[...] = carry


def _inproj(x3, g, w_in, b_f):
    b, s, d = x3.shape
    gw = GROUP_WIDTH
    nf = HEADS_PER_GROUP
    starts = (0, 3 * gw + nf, 6 * gw + nf, 9 * gw + nf)
    w_qk = jnp.concatenate([w_in[:, st:st + 2 * gw] for st in starts], axis=1).astype(BF16)
    w_vt = jnp.concatenate([w_in[:, st + 2 * gw:st + 3 * gw] for st in starts], axis=1).T.astype(BF16)
    rep = 2 * N_SPLIT
    w_f = jnp.pad(jnp.repeat(w_in[:, 3 * gw:3 * gw + nf], rep, axis=1),
                  ((0, 0), (0, LANES - rep * nf))).astype(BF16)
    b_col = jnp.pad(jnp.repeat(b_f, rep), (0, LANES - rep * nf)).reshape(1, LANES)
    width = w_qk.shape[1]
    cs = np.ones((1, width), np.float32)
    for grp in range(N_GROUPS):
        scale = DIFF_QK_DIM ** -0.5 if grp == 2 else HEAD_DIM ** -0.5
        cs[0, 2 * grp * gw:(2 * grp + 1) * gw] = scale * LOG2E
    tm = min(TOKEN_TILE, s)
    blk = min(PREFIX_BLOCK, tm)
    return pl.pallas_call(
        functools.partial(_inproj_kernel, tm=tm, blk=blk),
        grid=(b, s // tm),
        in_specs=[
            pl.BlockSpec((1, tm, d), lambda bi, j: (bi, j, 0)),
            _resident((1, d)),
            _resident((d, width)),
            _resident((1, width)),
            _resident((N_GROUPS * gw, d)),
            _resident((d, LANES)),
            _resident((1, LANES)),
        ],
        out_specs=[
            pl.BlockSpec((1, tm, width), lambda bi, j: (bi, j, 0)),
            pl.BlockSpec((1, N_GROUPS * gw, tm), lambda bi, j: (bi, 0, j)),
            pl.BlockSpec((1, tm, LANES), lambda bi, j: (bi, j, 0)),
            pl.BlockSpec((1, tm, LANES), lambda bi, j: (bi, j, 0)),
        ],
        out_shape=[
            jax.ShapeDtypeStruct((b, s, width), BF16),
            jax.ShapeDtypeStruct((b, N_GROUPS * gw, s), BF16),
            jax.ShapeDtypeStruct((b, s, LANES), BF16),
            jax.ShapeDtypeStruct((b, s, LANES), BF16),
        ],
        scratch_shapes=[pltpu.VMEM((1, LANES), F32)],
        compiler_params=_params(("arbitrary", "arbitrary")),
        name="inproj",
    )(x3, g.reshape(1, d), w_qk, jnp.asarray(cs), w_vt, w_f, b_col)


def _lane_range(lo, width):
    lane = lax.broadcasted_iota(jnp.int32, (1, LANES), 1)
    return (lane >= lo) & (lane < lo + width)


def _query_rows(ref, q_tile, t):
    return ref[0, pl.ds(pl.multiple_of(q_tile * t, t), t), :]


def _masked_queries(q, width):
    out = []
    for pair in range(2):
        q_pair = q[:, pair * LANES:(pair + 1) * LANES]
        for lo in range(0, LANES, width):
            out.append(jnp.where(_lane_range(lo, width), q_pair, jnp.zeros_like(q_pair)))
    return out


def _head_scores(k_ref, qs, t, per_head, k_extra=None):
    def scores(kb, cs):
        k0 = pl.multiple_of(kb * t, t)
        out = []
        for c in cs:
            pair = c // (2 * per_head)
            k = k_ref[0, pl.ds(k0, t), pair * LANES:(pair + 1) * LANES]
            if k_extra is not None:
                k = jnp.concatenate([k, k_extra(c, k0)], axis=1)
            out.append(lax.dot_general(k, qs[c], _NT, preferred_element_type=F32))
        return out
    return scores


def _head_products(vt_ref, t, per_head, with_ones=True):
    def products(ps, kb, cs):
        k0 = pl.multiple_of(kb * t, t)
        out = []
        for c, p in zip(cs, ps):
            h = c // per_head
            vt = vt_ref[0, h * HEAD_DIM:(h + 1) * HEAD_DIM, pl.ds(k0, t)]
            if with_ones:
                vt = jnp.concatenate([vt, jnp.ones((ONES_ROWS, t), vt.dtype)], axis=0)
            out.append(jnp.dot(vt, p, preferred_element_type=F32))
        return out
    return products


_MAX, _ALPHA, _OFFSET = range(3)


def _flash_forward(qi, n_q, t, n_chain, group, tile_ops, products,
                   s_scr, p_scr, acc_scr, st_scr, trips_per_iter=1):
    chains = range(n_chain)
    kb_lo, kb_hi, scores, adjust, shift_of = tile_ops(qi)

    def land(kb, tiles, diag, cs=chains, adjust=adjust, shift_of=shift_of):
        slot = kb & 1
        for c, tile in zip(cs, tiles):
            s = adjust(c, tile, kb, diag)
            s_scr[slot, c] = s
            mx = jnp.max(s, axis=0, keepdims=True)
            shift = shift_of(c, kb, diag)
            if shift is not None:
                mx = mx + shift
            m_old = st_scr[_MAX, c]
            m_new = jnp.maximum(m_old, mx)
            st_scr[_MAX, c] = m_new
            st_scr[_ALPHA, c] = jnp.exp2(m_old - m_new)
            st_scr[_OFFSET, c] = m_new if shift is None else m_new - shift

    def weigh(kb, cs=chains):
        slot = kb & 1
        for c in cs:
            r = st_scr[_OFFSET, c]
            for lo in range(0, t, LANES):
                cols = slice(lo, lo + LANES)
                p_scr[c, :, cols] = jnp.exp2(s_scr[slot, c, :, cols] - r[:, cols]).astype(BF16)
        return products([p_scr[c] for c in cs], kb, cs)

    def step(kb, diag_next):
        for g0 in range(0, n_chain, group):
            cs = range(g0, min(g0 + group, n_chain))
            alpha = [st_scr[_ALPHA, c] for c in cs]
            nxt = scores(kb + 1, cs)
            prod = weigh(kb, cs)
            land(kb + 1, nxt, diag_next, cs)
            for c, a, pr in zip(cs, alpha, prod):
                acc_scr[c] = a * acc_scr[c] + pr

    def reset_stats():
        for c in chains:
            st_scr[_MAX, c] = jnp.full((1, t), NEG, F32)

    for c in chains:
        acc_scr[c] = jnp.zeros(acc_scr.shape[1:], F32)

    @pl.when(qi == 0)
    def _():
        reset_stats()
        land(kb_lo, scores(kb_lo, chains), True)

    n_regular = jnp.maximum(kb_hi - 1 - kb_lo, 0)

    def body(i, carry):
        for u in range(trips_per_iter):
            step(kb_lo + trips_per_iter * i + u, False)
        return carry

    lax.fori_loop(0, n_regular // trips_per_iter, body, 0)
    for u in range(trips_per_iter - 1, 0, -1):

        @pl.when(n_regular % trips_per_iter >= u)
        def _(u=u):
            step(kb_hi - 1 - u, False)

    @pl.when(kb_hi > kb_lo)
    def _():
        step(kb_hi - 1, True)

    alpha = [st_scr[_ALPHA, c] for c in chains]
    prod = weigh(kb_hi)
    nxt_lo, _, nxt_scores, nxt_adjust, nxt_shift = tile_ops(jnp.minimum(qi + 1, n_q - 1))
    reset_stats()
    land(nxt_lo, nxt_scores(nxt_lo, chains), False, chains, nxt_adjust, nxt_shift)
    return [alpha[c] * acc_scr[c] + prod[c] for c in chains]


def _normalized(acc):
    return acc[:HEAD_DIM] * (1.0 / acc[HEAD_DIM:HEAD_DIM + 1])


def _store_pair(o_ref, pair, top, bottom):
    o_t = jnp.concatenate([top, bottom], axis=0)
    o_ref[0, :, pair * LANES:(pair + 1) * LANES] = o_t.T.astype(o_ref.dtype)


def _store_normalized(o_ref, acc):
    for pair in range(2):
        _store_pair(o_ref, pair, _normalized(acc[2 * pair]), _normalized(acc[2 * pair + 1]))


def _tile_iotas(t):
    key = lax.broadcasted_iota(jnp.int32, (t, t), 0)
    qry = lax.broadcasted_iota(jnp.int32, (t, t), 1)
    return key, qry


def _no_shift(c, kb, diag):
    return None


def _fox_kernel(q_ref, k_ref, vt_ref, qaug_ref, kaug_ref, o_ref, s_scr, p_scr, acc_scr, st_scr,
                *, t):
    rep = 2 * N_SPLIT

    def k_extra(h, k0):
        return kaug_ref[0, pl.ds(k0, t), :]

    def adjust(h, s, kb, diag):
        if not diag:
            return s
        key, qry = _tile_iotas(t)
        return jnp.where(key <= qry, s, NEG)

    def tile_ops(q_tile):
        q_aug = _query_rows(qaug_ref, q_tile, t)
        qs = [jnp.concatenate(
                  [q, jnp.where(_lane_range(rep * h, rep), q_aug, jnp.zeros_like(q_aug))], axis=1)
              for h, q in enumerate(_masked_queries(_query_rows(q_ref, q_tile, t), HEAD_DIM))]
        return 0, q_tile, _head_scores(k_ref, qs, t, 1, k_extra), adjust, _no_shift

    acc = _flash_forward(pl.program_id(1), pl.num_programs(1), t, HEADS_PER_GROUP, 2 * CHAIN_GROUP,
                         tile_ops, _head_products(vt_ref, t, 1), s_scr, p_scr, acc_scr, st_scr,
                         trips_per_iter=2)
    _store_normalized(o_ref, acc)


def _band_kernel(q_ref, k_ref, vt_ref, bias_ref, o_ref, s_scr, p_scr, acc_scr, st_scr,
                 *, t, reach):
    def tile_ops(q_tile):
        qs = _masked_queries(_query_rows(q_ref, q_tile, t), HEAD_DIM)

        def adjust(h, s, kb, diag):
            return s + bias_ref[h, kb - q_tile + reach]

        return (jnp.maximum(q_tile - reach, 0), q_tile, _head_scores(k_ref, qs, t, 1), adjust,
                _no_shift)

    acc = _flash_forward(pl.program_id(1), pl.num_programs(1), t, HEADS_PER_GROUP, CHAIN_GROUP,
                         tile_ops, _head_products(vt_ref, t, 1), s_scr, p_scr, acc_scr, st_scr,
                         trips_per_iter=2)
    _store_normalized(o_ref, acc)


def _alibi_lanes(t):
    q_aug = np.zeros((HEADS_PER_GROUP, t, LANES), np.float32)
    k_aug = np.zeros((HEADS_PER_GROUP, t, LANES), np.float32)
    off = np.arange(t, dtype=np.float32)
    for h in range(HEADS_PER_GROUP):
        parts = _split3_const(_alibi_slope(h) * LOG2E)
        for i, c in enumerate(parts):
            q_aug[h, :, i] = off
            k_aug[h, :, i] = -c
            q_aug[h, :, N_SPLIT + i] = c
            k_aug[h, :, N_SPLIT + i] = off
    return jnp.asarray(q_aug, BF16), jnp.asarray(k_aug, BF16)


def _alibi_slope(h):
    return 2.0 ** (-8.0 * (h + 1) / HEADS_PER_GROUP)


def _alibi_diag_fix(t):
    key = np.arange(t)[:, None]
    qry = np.arange(t)[None, :]
    dist = np.minimum(qry - key, 0).astype(np.float32)
    fix = np.stack([dist * np.float32(2.0 * _alibi_slope(h) * LOG2E) for h in range(HEADS_PER_GROUP)])
    return jnp.asarray(np.where((key // CHUNK <= qry // CHUNK)[None], fix, np.float32(NEG)))


def _diff_kernel(q_ref, k_ref, vt_ref, lam_ref, qal_ref, kal_ref, fix_ref, o_ref,
                 s_scr, p_scr, acc_scr, st_scr, *, t, lam_init):
    qi = pl.program_id(1)
    lp = lam_ref[...]
    lam = (jnp.exp(jnp.sum(lp[0:1] * lp[1:2], axis=-1, keepdims=True))
           - jnp.exp(jnp.sum(lp[2:3] * lp[3:4], axis=-1, keepdims=True)) + lam_init)
    def k_extra(c, k0):
        return kal_ref[c // 2]

    def adjust(c, s, kb, diag):
        return s + fix_ref[c // 2] if diag else s

    def tile_ops(q_tile):
        qs = [jnp.concatenate([q, qal_ref[c // 2]], axis=1)
              for c, q in enumerate(_masked_queries(_query_rows(q_ref, q_tile, t), DIFF_QK_DIM))]

        def shift_of(c, kb, diag):
            if diag:
                return None
            return (-_alibi_slope(c // 2) * LOG2E * t) * (q_tile - kb).astype(F32)

        return 0, q_tile, _head_scores(k_ref, qs, t, 2, k_extra), adjust, shift_of

    n_chain = 2 * HEADS_PER_GROUP
    acc = _flash_forward(qi, pl.num_programs(1), t, n_chain, 2 * CHAIN_GROUP, tile_ops,
                         _head_products(vt_ref, t, 2), s_scr, p_scr, acc_scr, st_scr)
    for pair in range(2):
        halves = []
        for hh in range(2):
            h = 2 * pair + hh
            o_h = _normalized(acc[2 * h]) - lam * _normalized(acc[2 * h + 1])
            ms = jnp.mean(o_h * o_h, axis=0, keepdims=True)
            halves.append(o_h * lax.rsqrt(ms + RMS_EPS) * (1.0 - lam_init))
        _store_pair(o_ref, pair, halves[0], halves[1])


def _softplus2(z):
    neg_abs = pltpu.bitcast(pltpu.bitcast(z, jnp.uint32) | jnp.uint32(0x80000000), F32)
    return jnp.maximum(z, 0.0) + jnp.log(1.0 + jnp.exp2(neg_abs)) * LOG2E


def _stick_kernel(q_ref, k_ref, vt_ref, o_ref, z_scr, w_scr, acc_scr, st_scr, wi_scr, *, t):
    qi = pl.program_id(1)
    chains = range(HEADS_PER_GROUP)
    products = _head_products(vt_ref, t, 1, with_ones=False)
    tail_scr = st_scr.at[0]

    scores = _head_scores(k_ref, _masked_queries(_query_rows(q_ref, qi, t), HEAD_DIM), t, 1)

    def land(j, tiles, diag, cs=chains):
        slot = j & 1
        key, qry = _tile_iotas(t)
        minus_suffix = jnp.where(qry >= key, -1.0, 0.0).astype(BF16)
        within = []
        for c, z in zip(cs, tiles):
            rest = _softplus2(z)
            if diag:
                rest = jnp.where(key < qry, rest, 0.0)
                z = jnp.where(key < qry, z, NEG)
            z_scr[slot, c] = z
            hi = rest.astype(BF16)
            lo = (rest - hi.astype(F32)).astype(BF16)
            within.append(jnp.dot(minus_suffix, hi, preferred_element_type=F32)
                          + jnp.dot(minus_suffix, lo, preferred_element_type=F32))
        return within

    def weigh(j, cs=chains):
        slot = j & 1
        for c in cs:
            tail = tail_scr[c]
            for lo in range(0, t, LANES):
                cols = slice(lo, lo + LANES)
                logw = z_scr[slot, c, :, cols] + wi_scr[c, :, cols] + tail[:, cols]
                w_scr[c, :, cols] = jnp.exp2(logw).astype(BF16)
            tail_scr[c] = tail + wi_scr[c, 0:1, :]
        return products([w_scr[c] for c in cs], qi - j, cs)

    def trip(j):
        nxt = scores(qi - j - 1, chains)
        prod = weigh(j)
        within = land(j + 1, nxt, False)
        for c in chains:
            acc_scr[c] = acc_scr[c] + prod[c]
            wi_scr[c] = within[c]

    def body(i, carry):
        trip(2 * i)
        trip(2 * i + 1)
        return carry

    for c, wi in enumerate(land(0, scores(qi, chains), True)):
        tail_scr[c] = jnp.zeros((1, t), F32)
        acc_scr[c] = jnp.zeros(acc_scr.shape[1:], F32)
        wi_scr[c] = wi
    lax.fori_loop(0, qi // 2, body, 0)

    @pl.when(qi % 2 == 1)
    def _():
        trip(qi - 1)
    prod = weigh(qi)
    acc = [acc_scr[c] + prod[c] for c in chains]
    for pair in range(2):
        _store_pair(o_ref, pair, acc[2 * pair], acc[2 * pair + 1])


def _mixer_call(body, qk, vt, group, n_chain, extra_inputs, extra_specs, name,
                acc_rows=HEAD_DIM + ONES_ROWS, extra_scratch=()):
    b, s, _ = qk.shape
    t = min(ATTN_TILE, s)
    gw = GROUP_WIDTH
    return pl.pallas_call(
        functools.partial(body, t=t),
        grid=(b, s // t),
        in_specs=[
            pl.BlockSpec((1, s, gw), lambda bi, qi: (bi, 0, 2 * group)),
            pl.BlockSpec((1, s, gw), lambda bi, qi: (bi, 0, 2 * group + 1)),
            pl.BlockSpec((1, gw, s), lambda bi, qi: (bi, group, 0)),
        ] + extra_specs,
        out_specs=pl.BlockSpec((1, t, gw), lambda bi, qi: (bi, qi, 0)),
        out_shape=jax.ShapeDtypeStruct((b, s, gw), BF16),
        scratch_shapes=[pltpu.VMEM((2, n_chain, t, t), F32),
                        pltpu.VMEM((n_chain, t, t), BF16),
                        pltpu.VMEM((n_chain, acc_rows, t), F32),
                        pltpu.VMEM((3, n_chain, 1, t), F32)]
                       + list(extra_scratch),
        compiler_params=_params(("arbitrary", "arbitrary")),
        name=name,
    )(qk, qk, vt, *extra_inputs)


def _band_bias(rel_table, t, reach):
    n_keys = (reach + 1) * t
    period = n_keys + t
    d = np.arange(period)
    d = np.where(d < t, d, d - period)
    idx = np.clip(reach * t + d, -MAX_REL_DIST, MAX_REL_DIST) + MAX_REL_DIST
    ext = rel_table.astype(F32)[:, idx] * LOG2E
    h = rel_table.shape[0]
    rows = jnp.tile(ext, (1, n_keys))[:, :n_keys * (period - 1)].reshape(h, n_keys, period - 1)
    bias = rows[:, :, :t]
    kc = (np.arange(n_keys) // CHUNK - (reach * t // CHUNK - BAND_CHUNKS))[:, None]
    qc = (np.arange(t) // CHUNK)[None, :]
    visible = (kc >= qc) & (kc <= qc + BAND_CHUNKS)
    return jnp.where(visible[None], bias, NEG).reshape(h, reach + 1, t, t)


def _mixers(qk, vt, kaug, qaug, rel_table, lam_params, lam_init):
    b, s, _ = qk.shape
    t = min(ATTN_TILE, s)
    assert s % t == 0 and t % CHUNK == 0 and t % LANES == 0
    reach = -(-BAND_CHUNKS * CHUNK // t)
    nh = HEADS_PER_GROUP
    o_a = _mixer_call(
        _fox_kernel, qk, vt, 0, nh, [qaug, kaug],
        [pl.BlockSpec((1, s, LANES), lambda bi, qi: (bi, 0, 0)),
         pl.BlockSpec((1, s, LANES), lambda bi, qi: (bi, 0, 0))], "fox")
    bias = _band_bias(rel_table, t, reach)
    o_b = _mixer_call(
        functools.partial(_band_kernel, reach=reach), qk, vt, 1, nh, [bias],
        [_resident(bias.shape)], "band")
    q_al, k_al = _alibi_lanes(t)
    fix = _alibi_diag_fix(t)
    o_c = _mixer_call(
        functools.partial(_diff_kernel, lam_init=lam_init), qk, vt, 2, 2 * nh,
        [lam_params, q_al, k_al, fix],
        [_resident(lam_params.shape), _resident(q_al.shape), _resident(k_al.shape),
         _resident(fix.shape)], "diff")
    o_d = _mixer_call(_stick_kernel, qk, vt, 3, nh, [], [], "stick", acc_rows=HEAD_DIM,
                      extra_scratch=[pltpu.VMEM((nh, t, t), F32)])
    return o_a, o_b, o_c, o_d


def _lambda_init(layer_idx):
    return 0.8 - 0.6 * math.exp(-0.3 * layer_idx)


def kernel(x, g_ffn1, ffn1_w_gu, ffn1_w_down, g_mix, w_in, b_f, rel_bias, diff_lambda, w_out,
           g_ffn2, ffn2_w_gu, ffn2_w_down, g_final):
    b, s, d = x.shape
    depth = g_ffn1.shape[0]
    x2 = x.reshape(b * s, d)
    for layer in range(depth):
        x2 = _ffn(x2, g_ffn1[layer], ffn1_w_gu[layer], ffn1_w_down[layer])
        qk, vt, kaug, qaug = _inproj(x2.reshape(b, s, d), g_mix[layer], w_in[layer], b_f[layer])
        mixed = _mixers(qk, vt, kaug, qaug, rel_bias[layer], diff_lambda[layer], _lambda_init(layer))
        x2 = _ffn(x2, g_ffn2[layer], ffn2_w_gu[layer], ffn2_w_down[layer], mixed, w_out[layer],
                  g_final if layer == depth - 1 else None)
    return x2.reshape(b, s, d)
```

```python
import functools
import math

import numpy as np
import jax
import jax.numpy as jnp
from jax import lax
from jax.experimental import pallas as pl
from jax.experimental.pallas import tpu as pltpu

F32 = jnp.float32
BF16 = jnp.bfloat16

LANES = 128
HEAD_DIM = 64
HEADS_PER_GROUP = 4
GROUP_WIDTH = HEADS_PER_GROUP * HEAD_DIM
N_GROUPS = 4
CHUNK = 64
BAND_CHUNKS = 8
MAX_REL_DIST = 256
DIFF_QK_DIM = HEAD_DIM // 2
FFN_RES = 0.5
RMS_EPS = 1e-6
NEG = -1e30
LOG2E = math.log2(math.e)
N_SPLIT = 3
ONES_ROWS = 16

TOKEN_TILE = 512
ATTN_TILE = 256
CHAIN_GROUP = 2
MXU_DIM = 256
FFN_CHUNK = 6 * MXU_DIM
PREFIX_BLOCK = MXU_DIM
VMEM_LIMIT = 56 * 1024 * 1024

_NT = (((1,), (1,)), ((), ()))


def _rmsnorm(x, g):
    y = x * lax.rsqrt(jnp.mean(x * x, axis=-1, keepdims=True) + RMS_EPS)
    return y * g


def _log_sigmoid(x):
    return jnp.minimum(x, 0.0) - jnp.log1p(jnp.exp(-jnp.abs(x)))


def _split3(x):
    h1 = x.astype(BF16)
    r1 = x - h1.astype(F32)
    h2 = r1.astype(BF16)
    h3 = (r1 - h2.astype(F32)).astype(BF16)
    return h1, h2, h3


def _split3_const(c):
    parts = []
    for _ in range(N_SPLIT):
        p = float(np.asarray(c, np.float32).astype(jnp.bfloat16).astype(np.float32))
        parts.append(p)
        c = c - p
    return parts


def _params(sem):
    return pltpu.CompilerParams(dimension_semantics=sem, vmem_limit_bytes=VMEM_LIMIT)


def _resident(shape):
    zeros = (0,) * len(shape)
    return pl.BlockSpec(shape, lambda *_: zeros, pipeline_mode=pl.Buffered(1))


def _ffn_kernel(*refs, d_ff, chunk, n_mixed, final_norm):
    x_ref, refs = refs[0], refs[1:]
    x = x_ref[...]
    if n_mixed:
        wout_ref = refs[n_mixed]
        width = wout_ref.shape[0] // n_mixed
        for grp, m_ref in enumerate(refs[:n_mixed]):
            x = x + jnp.dot(m_ref[...], wout_ref[grp * width:(grp + 1) * width, :],
                            preferred_element_type=F32)
        refs = refs[n_mixed + 1:]
    g_ref, wgu_ref, wd_ref = refs[:3]
    o_ref = refs[-1]
    h = _rmsnorm(x, g_ref[...]).astype(BF16)
    acc = jnp.zeros(x.shape, F32)
    for c0 in range(0, d_ff, chunk):
        c1 = min(c0 + chunk, d_ff)
        gate = jnp.dot(h, wgu_ref[:, c0:c1], preferred_element_type=F32)
        up = jnp.dot(h, wgu_ref[:, d_ff + c0:d_ff + c1], preferred_element_type=F32)
        act = gate * jax.nn.sigmoid(gate) * up
        acc = acc + jnp.dot(act.astype(BF16), wd_ref[c0:c1, :], preferred_element_type=F32)
    y = x + FFN_RES * acc
    o_ref[...] = _rmsnorm(y, refs[3][...]) if final_norm else y


def _ffn(x2, g, w_gu, w_down, mixed=(), w_out=None, g_final=None):
    n, d = x2.shape
    d_ff = w_down.shape[0]
    chunk = FFN_CHUNK
    tm = min(TOKEN_TILE, n)
    row_spec = pl.BlockSpec((tm, d), lambda i: (i, 0))
    args, specs = [x2], [row_spec]
    if mixed:
        for m in mixed:
            args.append(m.reshape(n, m.shape[-1]))
            specs.append(pl.BlockSpec((tm, m.shape[-1]), lambda i: (i, 0)))
        args.append(w_out.astype(BF16))
        specs.append(_resident(w_out.shape))
    args += [g.reshape(1, d), w_gu.astype(BF16), w_down.astype(BF16)]
    specs += [_resident((1, d)), _resident((d, 2 * d_ff)), _resident((d_ff, d))]
    if g_final is not None:
        args.append(g_final.reshape(1, d))
        specs.append(_resident((1, d)))
    return pl.pallas_call(
        functools.partial(_ffn_kernel, d_ff=d_ff, chunk=chunk, n_mixed=len(mixed),
                          final_norm=g_final is not None),
        grid=(n // tm,),
        in_specs=specs,
        out_specs=row_spec,
        out_shape=jax.ShapeDtypeStruct((n, d), F32),
        compiler_params=_params(("arbitrary",)),
        name="ffn",
    )(*args)


def _inproj_kernel(x_ref, g_ref, wqk_ref, cs_ref, wvt_ref, wf_ref, bf_ref,
                   qk_ref, vt_ref, kaug_ref, qaug_ref, carry_ref, *, tm, blk):
    @pl.when(pl.program_id(1) == 0)
    def _():
        carry_ref[...] = jnp.zeros_like(carry_ref)

    h = _rmsnorm(x_ref[0], g_ref[...]).astype(BF16)
    qk = jnp.dot(h, wqk_ref[...], preferred_element_type=F32)
    qk_ref[0] = (qk * cs_ref[...]).astype(BF16)
    vt_ref[0] = lax.dot_general(wvt_ref[...], h, _NT, preferred_element_type=F32).astype(BF16)

    ri = lax.broadcasted_iota(jnp.int32, (blk, blk), 0)
    ci = lax.broadcasted_iota(jnp.int32, (blk, blk), 1)
    tril = jnp.where(ci <= ri, 1.0, 0.0).astype(BF16)
    ls = _log_sigmoid(jnp.dot(h, wf_ref[...], preferred_element_type=F32) + bf_ref[...])
    lane = lax.broadcasted_iota(jnp.int32, (1, LANES), 1)
    used = lane < 2 * N_SPLIT * HEADS_PER_GROUP
    slot = lane % (2 * N_SPLIT)
    carry = carry_ref[...]
    for r0 in range(0, tm, blk):
        cum = carry
        for part in _split3(ls[r0:r0 + blk]):
            cum = cum + jnp.dot(tril, part, preferred_element_type=F32)
        carry = cum[blk - 1:blk, :]
        kaug = jnp.where(slot >= N_SPLIT, 1.0, 0.0)
        qaug = jnp.where(slot < N_SPLIT, -1.0, 0.0)
        for i, part in enumerate(_split3(cum * LOG2E)):
            kaug = jnp.where(slot == i, part.astype(F32), kaug)
            qaug = jnp.where(slot == N_SPLIT + i, part.astype(F32), qaug)
        kaug_ref[0, r0:r0 + blk, :] = jnp.where(used, kaug, 0.0).astype(BF16)
        qaug_ref[0, r0:r0 + blk, :] = jnp.where(used, qaug, 0.0).astype(BF16)
    carry_ref[...] = carry


def _inproj(x3, g, w_in, b_f):
    b, s, d = x3.shape
    gw = GROUP_WIDTH
    nf = HEADS_PER_GROUP
    starts = (0, 3 * gw + nf, 6 * gw + nf, 9 * gw + nf)
    w_qk = jnp.concatenate([w_in[:, st:st + 2 * gw] for st in starts], axis=1).astype(BF16)
    w_vt = jnp.concatenate([w_in[:, st + 2 * gw:st + 3 * gw] for st in starts], axis=1).T.astype(BF16)
    rep = 2 * N_SPLIT
    w_f = jnp.pad(jnp.repeat(w_in[:, 3 * gw:3 * gw + nf], rep, axis=1),
                  ((0, 0), (0, LANES - rep * nf))).astype(BF16)
    b_col = jnp.pad(jnp.repeat(b_f, rep), (0, LANES - rep * nf)).reshape(1, LANES)
    width = w_qk.shape[1]
    cs = np.ones((1, width), np.float32)
    for grp in range(N_GROUPS):
        scale = DIFF_QK_DIM ** -0.5 if grp == 2 else HEAD_DIM ** -0.5
        cs[0, 2 * grp * gw:(2 * grp + 1) * gw] = scale * LOG2E
    tm = min(TOKEN_TILE, s)
    blk = min(PREFIX_BLOCK, tm)
    return pl.pallas_call(
        functools.partial(_inproj_kernel, tm=tm, blk=blk),
        grid=(b, s // tm),
        in_specs=[
            pl.BlockSpec((1, tm, d), lambda bi, j: (bi, j, 0)),
            _resident((1, d)),
            _resident((d, width)),
            _resident((1, width)),
            _resident((N_GROUPS * gw, d)),
            _resident((d, LANES)),
            _resident((1, LANES)),
        ],
        out_specs=[
            pl.BlockSpec((1, tm, width), lambda bi, j: (bi, j, 0)),
            pl.BlockSpec((1, N_GROUPS * gw, tm), lambda bi, j: (bi, 0, j)),
            pl.BlockSpec((1, tm, LANES), lambda bi, j: (bi, j, 0)),
            pl.BlockSpec((1, tm, LANES), lambda bi, j: (bi, j, 0)),
        ],
        out_shape=[
            jax.ShapeDtypeStruct((b, s, width), BF16),
            jax.ShapeDtypeStruct((b, N_GROUPS * gw, s), BF16),
            jax.ShapeDtypeStruct((b, s, LANES), BF16),
            jax.ShapeDtypeStruct((b, s, LANES), BF16),
        ],
        scratch_shapes=[pltpu.VMEM((1, LANES), F32)],
        compiler_params=_params(("arbitrary", "arbitrary")),
        name="inproj",
    )(x3, g.reshape(1, d), w_qk, jnp.asarray(cs), w_vt, w_f, b_col)


def _lane_range(lo, width):
    lane = lax.broadcasted_iota(jnp.int32, (1, LANES), 1)
    return (lane >= lo) & (lane < lo + width)


def _query_rows(ref, q_tile, t):
    return ref[0, pl.ds(pl.multiple_of(q_tile * t, t), t), :]


def _masked_queries(q, width):
    out = []
    for pair in range(2):
        q_pair = q[:, pair * LANES:(pair + 1) * LANES]
        for lo in range(0, LANES, width):
            out.append(jnp.where(_lane_range(lo, width), q_pair, jnp.zeros_like(q_pair)))
    return out


def _head_scores(k_ref, qs, t, per_head, k_extra=None):
    def scores(kb, cs):
        k0 = pl.multiple_of(kb * t, t)
        out = []
        for c in cs:
            pair = c // (2 * per_head)
            k = k_ref[0, pl.ds(k0, t), pair * LANES:(pair + 1) * LANES]
            if k_extra is not None:
                k = jnp.concatenate([k, k_extra(c, k0)], axis=1)
            out.append(lax.dot_general(k, qs[c], _NT, preferred_element_type=F32))
        return out
    return scores


def _head_products(vt_ref, t, per_head, with_ones=True):
    def products(ps, kb, cs):
        k0 = pl.multiple_of(kb * t, t)
        out = []
        for c, p in zip(cs, ps):
            h = c // per_head
            vt = vt_ref[0, h * HEAD_DIM:(h + 1) * HEAD_DIM, pl.ds(k0, t)]
            if with_ones:
                vt = jnp.concatenate([vt, jnp.ones((ONES_ROWS, t), vt.dtype)], axis=0)
            out.append(jnp.dot(vt, p, preferred_element_type=F32))
        return out
    return products


_MAX, _ALPHA, _OFFSET = range(3)


def _flash_forward(qi, n_q, t, n_chain, group, tile_ops, products, emit,
                   s_scr, p_scr, acc_scr, st_scr, trips_per_iter=1):
    chains = range(n_chain)
    kb_lo, kb_hi, scores, adjust, shift_of = tile_ops(qi)

    def land(kb, tiles, diag, cs=chains, adjust=adjust, shift_of=shift_of):
        slot = kb & 1
        for c, tile in zip(cs, tiles):
            s = adjust(c, tile, kb, diag)
            s_scr[slot, c] = s
            mx = jnp.max(s, axis=0, keepdims=True)
            shift = shift_of(c, kb, diag)
            if shift is not None:
                mx = mx + shift
            m_old = st_scr[_MAX, c]
            m_new = jnp.maximum(m_old, mx)
            st_scr[_MAX, c] = m_new
            st_scr[_ALPHA, c] = jnp.exp2(m_old - m_new)
            st_scr[_OFFSET, c] = m_new if shift is None else m_new - shift

    def weigh(kb, cs=chains):
        slot = kb & 1
        for c in cs:
            r = st_scr[_OFFSET, c]
            for lo in range(0, t, LANES):
                cols = slice(lo, lo + LANES)
                p_scr[c, :, cols] = jnp.exp2(s_scr[slot, c, :, cols] - r[:, cols]).astype(BF16)
        return products([p_scr[c] for c in cs], kb, cs)

    def step(kb, diag_next):
        for g0 in range(0, n_chain, group):
            cs = range(g0, min(g0 + group, n_chain))
            alpha = [st_scr[_ALPHA, c] for c in cs]
            nxt = scores(kb + 1, cs)
            prod = weigh(kb, cs)
            land(kb + 1, nxt, diag_next, cs)
            for c, a, pr in zip(cs, alpha, prod):
                acc_scr[c] = a * acc_scr[c] + pr

    def reset_stats():
        for c in chains:
            st_scr[_MAX, c] = jnp.full((1, t), NEG, F32)

    def finish():
        alpha = [st_scr[_ALPHA, c] for c in chains]
        prod = weigh(kb_hi)
        nxt_lo, _, nxt_scores, nxt_adjust, nxt_shift = tile_ops(jnp.minimum(qi + 1, n_q - 1))
        reset_stats()
        land(nxt_lo, nxt_scores(nxt_lo, chains), False, chains, nxt_adjust, nxt_shift)
        emit([alpha[c] * acc_scr[c] + prod[c] for c in chains])

    for c in chains:
        acc_scr[c] = jnp.zeros(acc_scr.shape[1:], F32)

    @pl.when(qi == 0)
    def _():
        reset_stats()
        land(kb_lo, scores(kb_lo, chains), True)
        finish()

    n_regular = jnp.maximum(kb_hi - 1 - kb_lo, 0)

    def body(i, carry):
        for u in range(trips_per_iter):
            step(kb_lo + trips_per_iter * i + u, False)
        return carry

    lax.fori_loop(0, n_regular // trips_per_iter, body, 0)
    for u in range(trips_per_iter - 1, 0, -1):

        @pl.when(n_regular % trips_per_iter >= u)
        def _(u=u):
            step(kb_hi - 1 - u, False)

    @pl.when(qi > 0)
    def _():
        step(kb_hi - 1, True)
        finish()


def _normalized(acc):
    return acc[:HEAD_DIM] * (1.0 / acc[HEAD_DIM:HEAD_DIM + 1])


def _store_pair(o_ref, pair, top, bottom):
    o_t = jnp.concatenate([top, bottom], axis=0)
    o_ref[0, :, pair * LANES:(pair + 1) * LANES] = o_t.T.astype(o_ref.dtype)


def _store_normalized(o_ref, acc):
    for pair in range(2):
        _store_pair(o_ref, pair, _normalized(acc[2 * pair]), _normalized(acc[2 * pair + 1]))


def _tile_iotas(t):
    key = lax.broadcasted_iota(jnp.int32, (t, t), 0)
    qry = lax.broadcasted_iota(jnp.int32, (t, t), 1)
    return key, qry


def _no_shift(c, kb, diag):
    return None


def _fox_kernel(q_ref, k_ref, vt_ref, qaug_ref, kaug_ref, o_ref, s_scr, p_scr, acc_scr, st_scr,
                *, t):
    rep = 2 * N_SPLIT

    def k_extra(h, k0):
        return kaug_ref[0, pl.ds(k0, t), :]

    def adjust(h, s, kb, diag):
        if not diag:
            return s
        key, qry = _tile_iotas(t)
        return jnp.where(key <= qry, s, NEG)

    def tile_ops(q_tile):
        q_aug = _query_rows(qaug_ref, q_tile, t)
        qs = [jnp.concatenate(
                  [q, jnp.where(_lane_range(rep * h, rep), q_aug, jnp.zeros_like(q_aug))], axis=1)
              for h, q in enumerate(_masked_queries(_query_rows(q_ref, q_tile, t), HEAD_DIM))]
        return 0, q_tile, _head_scores(k_ref, qs, t, 1, k_extra), adjust, _no_shift

    _flash_forward(pl.program_id(1), pl.num_programs(1), t, HEADS_PER_GROUP, 2 * CHAIN_GROUP,
                   tile_ops, _head_products(vt_ref, t, 1),
                   functools.partial(_store_normalized, o_ref),
                   s_scr, p_scr, acc_scr, st_scr, trips_per_iter=2)


def _band_kernel(q_ref, k_ref, vt_ref, bias_ref, o_ref, s_scr, p_scr, acc_scr, st_scr,
                 *, t, reach):
    def tile_ops(q_tile):
        qs = _masked_queries(_query_rows(q_ref, q_tile, t), HEAD_DIM)

        def adjust(h, s, kb, diag):
            return s + bias_ref[h, kb - q_tile + reach]

        return (jnp.maximum(q_tile - reach, 0), q_tile, _head_scores(k_ref, qs, t, 1), adjust,
                _no_shift)

    _flash_forward(pl.program_id(1), pl.num_programs(1), t, HEADS_PER_GROUP, CHAIN_GROUP,
                   tile_ops, _head_products(vt_ref, t, 1),
                   functools.partial(_store_normalized, o_ref),
                   s_scr, p_scr, acc_scr, st_scr, trips_per_iter=2)


def _alibi_lanes(t):
    q_aug = np.zeros((HEADS_PER_GROUP, t, LANES), np.float32)
    k_aug = np.zeros((HEADS_PER_GROUP, t, LANES), np.float32)
    off = np.arange(t, dtype=np.float32)
    for h in range(HEADS_PER_GROUP):
        parts = _split3_const(_alibi_slope(h) * LOG2E)
        for i, c in enumerate(parts):
            q_aug[h, :, i] = off
            k_aug[h, :, i] = -c
            q_aug[h, :, N_SPLIT + i] = c
            k_aug[h, :, N_SPLIT + i] = off
    return jnp.asarray(q_aug, BF16), jnp.asarray(k_aug, BF16)


def _alibi_slope(h):
    return 2.0 ** (-8.0 * (h + 1) / HEADS_PER_GROUP)


def _alibi_diag_fix(t):
    key = np.arange(t)[:, None]
    qry = np.arange(t)[None, :]
    dist = np.minimum(qry - key, 0).astype(np.float32)
    fix = np.stack([dist * np.float32(2.0 * _alibi_slope(h) * LOG2E) for h in range(HEADS_PER_GROUP)])
    return jnp.asarray(np.where((key // CHUNK <= qry // CHUNK)[None], fix, np.float32(NEG)))


def _diff_kernel(q_ref, k_ref, vt_ref, lam_ref, qal_ref, kal_ref, fix_ref, o_ref,
                 s_scr, p_scr, acc_scr, st_scr, *, t, lam_init):
    qi = pl.program_id(1)
    lp = lam_ref[...]
    lam = (jnp.exp(jnp.sum(lp[0:1] * lp[1:2], axis=-1, keepdims=True))
           - jnp.exp(jnp.sum(lp[2:3] * lp[3:4], axis=-1, keepdims=True)) + lam_init)
    def k_extra(c, k0):
        return kal_ref[c // 2]

    def adjust(c, s, kb, diag):
        return s + fix_ref[c // 2] if diag else s

    def tile_ops(q_tile):
        qs = [jnp.concatenate([q, qal_ref[c // 2]], axis=1)
              for c, q in enumerate(_masked_queries(_query_rows(q_ref, q_tile, t), DIFF_QK_DIM))]

        def shift_of(c, kb, diag):
            if diag:
                return None
            return (-_alibi_slope(c // 2) * LOG2E * t) * (q_tile - kb).astype(F32)

        return 0, q_tile, _head_scores(k_ref, qs, t, 2, k_extra), adjust, shift_of

    n_chain = 2 * HEADS_PER_GROUP
    def emit(acc):
        for pair in range(2):
            halves = []
            for hh in range(2):
                h = 2 * pair + hh
                o_h = _normalized(acc[2 * h]) - lam * _normalized(acc[2 * h + 1])
                ms = jnp.mean(o_h * o_h, axis=0, keepdims=True)
                halves.append(o_h * lax.rsqrt(ms + RMS_EPS) * (1.0 - lam_init))
            _store_pair(o_ref, pair, halves[0], halves[1])

    _flash_forward(qi, pl.num_programs(1), t, n_chain, 2 * CHAIN_GROUP, tile_ops,
                   _head_products(vt_ref, t, 2), emit, s_scr, p_scr, acc_scr, st_scr)


def _softplus2(z):
    neg_abs = pltpu.bitcast(pltpu.bitcast(z, jnp.uint32) | jnp.uint32(0x80000000), F32)
    return jnp.maximum(z, 0.0) + jnp.log(1.0 + jnp.exp2(neg_abs)) * LOG2E


def _stick_kernel(q_ref, k_ref, vt_ref, o_ref, z_scr, w_scr, acc_scr, st_scr, wi_scr, *, t):
    qi = pl.program_id(1)
    chains = range(HEADS_PER_GROUP)
    products = _head_products(vt_ref, t, 1, with_ones=False)
    tail_scr = st_scr.at[0]

    scores = _head_scores(k_ref, _masked_queries(_query_rows(q_ref, qi, t), HEAD_DIM), t, 1)

    def land(j, tiles, diag, cs=chains):
        slot = j & 1
        key, qry = _tile_iotas(t)
        minus_suffix = jnp.where(qry >= key, -1.0, 0.0).astype(BF16)
        within = []
        for c, z in zip(cs, tiles):
            rest = _softplus2(z)
            if diag:
                rest = jnp.where(key < qry, rest, 0.0)
                z = jnp.where(key < qry, z, NEG)
            z_scr[slot, c] = z
            hi = rest.astype(BF16)
            lo = (rest - hi.astype(F32)).astype(BF16)
            within.append(jnp.dot(minus_suffix, hi, preferred_element_type=F32)
                          + jnp.dot(minus_suffix, lo, preferred_element_type=F32))
        return within

    def weigh(j, cs=chains):
        slot = j & 1
        for c in cs:
            tail = tail_scr[c]
            for lo in range(0, t, LANES):
                cols = slice(lo, lo + LANES)
                logw = z_scr[slot, c, :, cols] + wi_scr[c, :, cols] + tail[:, cols]
                w_scr[c, :, cols] = jnp.exp2(logw).astype(BF16)
            tail_scr[c] = tail + wi_scr[c, 0:1, :]
        return products([w_scr[c] for c in cs], qi - j, cs)

    def trip(j):
        nxt = scores(qi - j - 1, chains)
        prod = weigh(j)
        within = land(j + 1, nxt, False)
        for c in chains:
            acc_scr[c] = acc_scr[c] + prod[c]
            wi_scr[c] = within[c]

    def body(i, carry):
        trip(2 * i)
        trip(2 * i + 1)
        return carry

    for c, wi in enumerate(land(0, scores(qi, chains), True)):
        tail_scr[c] = jnp.zeros((1, t), F32)
        acc_scr[c] = jnp.zeros(acc_scr.shape[1:], F32)
        wi_scr[c] = wi
    lax.fori_loop(0, qi // 2, body, 0)

    def finish():
        prod = weigh(qi)
        acc = [acc_scr[c] + prod[c] for c in chains]
        for pair in range(2):
            _store_pair(o_ref, pair, acc[2 * pair], acc[2 * pair + 1])

    @pl.when(qi % 2 == 1)
    def _():
        trip(qi - 1)
        finish()

    @pl.when(qi % 2 == 0)
    def _():
        finish()


def _mixer_call(body, qk, vt, group, n_chain, extra_inputs, extra_specs, name,
                acc_rows=HEAD_DIM + ONES_ROWS, extra_scratch=()):
    b, s, _ = qk.shape
    t = min(ATTN_TILE, s)
    gw = GROUP_WIDTH
    return pl.pallas_call(
        functools.partial(body, t=t),
        grid=(b, s // t),
        in_specs=[
            pl.BlockSpec((1, s, gw), lambda bi, qi: (bi, 0, 2 * group)),
            pl.BlockSpec((1, s, gw), lambda bi, qi: (bi, 0, 2 * group + 1)),
            pl.BlockSpec((1, gw, s), lambda bi, qi: (bi, group, 0)),
        ] + extra_specs,
        out_specs=pl.BlockSpec((1, t, gw), lambda bi, qi: (bi, qi, 0)),
        out_shape=jax.ShapeDtypeStruct((b, s, gw), BF16),
        scratch_shapes=[pltpu.VMEM((2, n_chain, t, t), F32),
                        pltpu.VMEM((n_chain, t, t), BF16),
                        pltpu.VMEM((n_chain, acc_rows, t), F32),
                        pltpu.VMEM((3, n_chain, 1, t), F32)]
                       + list(extra_scratch),
        compiler_params=_params(("arbitrary", "arbitrary")),
        name=name,
    )(qk, qk, vt, *extra_inputs)


def _band_bias(rel_table, t, reach):
    n_keys = (reach + 1) * t
    period = n_keys + t
    d = np.arange(period)
    d = np.where(d < t, d, d - period)
    idx = np.clip(reach * t + d, -MAX_REL_DIST, MAX_REL_DIST) + MAX_REL_DIST
    ext = rel_table.astype(F32)[:, idx] * LOG2E
    h = rel_table.shape[0]
    rows = jnp.tile(ext, (1, n_keys))[:, :n_keys * (period - 1)].reshape(h, n_keys, period - 1)
    bias = rows[:, :, :t]
    kc = (np.arange(n_keys) // CHUNK - (reach * t // CHUNK - BAND_CHUNKS))[:, None]
    qc = (np.arange(t) // CHUNK)[None, :]
    visible = (kc >= qc) & (kc <= qc + BAND_CHUNKS)
    return jnp.where(visible[None], bias, NEG).reshape(h, reach + 1, t, t)


def _mixers(qk, vt, kaug, qaug, rel_table, lam_params, lam_init):
    b, s, _ = qk.shape
    t = min(ATTN_TILE, s)
    assert s % t == 0 and t % CHUNK == 0 and t % LANES == 0
    reach = -(-BAND_CHUNKS * CHUNK // t)
    nh = HEADS_PER_GROUP
    o_a = _mixer_call(
        _fox_kernel, qk, vt, 0, nh, [qaug, kaug],
        [pl.BlockSpec((1, s, LANES), lambda bi, qi: (bi, 0, 0)),
         pl.BlockSpec((1, s, LANES), lambda bi, qi: (bi, 0, 0))], "fox")
    bias = _band_bias(rel_table, t, reach)
    o_b = _mixer_call(
        functools.partial(_band_kernel, reach=reach), qk, vt, 1, nh, [bias],
        [_resident(bias.shape)], "band")
    q_al, k_al = _alibi_lanes(t)
    fix = _alibi_diag_fix(t)
    o_c = _mixer_call(
        functools.partial(_diff_kernel, lam_init=lam_init), qk, vt, 2, 2 * nh,
        [lam_params, q_al, k_al, fix],
        [_resident(lam_params.shape), _resident(q_al.shape), _resident(k_al.shape),
         _resident(fix.shape)], "diff")
    o_d = _mixer_call(_stick_kernel, qk, vt, 3, nh, [], [], "stick", acc_rows=HEAD_DIM,
                      extra_scratch=[pltpu.VMEM((nh, t, t), F32)])
    return o_a, o_b, o_c, o_d


def _lambda_init(layer_idx):
    return 0.8 - 0.6 * math.exp(-0.3 * layer_idx)


def kernel(x, g_ffn1, ffn1_w_gu, ffn1_w_down, g_mix, w_in, b_f, rel_bias, diff_lambda, w_out,
           g_ffn2, ffn2_w_gu, ffn2_w_down, g_final):
    b, s, d = x.shape
    depth = g_ffn1.shape[0]
    x2 = x.reshape(b * s, d)
    for layer in range(depth):
        x2 = _ffn(x2, g_ffn1[layer], ffn1_w_gu[layer], ffn1_w_down[layer])
        qk, vt, kaug, qaug = _inproj(x2.reshape(b, s, d), g_mix[layer], w_in[layer], b_f[layer])
        mixed = _mixers(qk, vt, kaug, qaug, rel_bias[layer], diff_lambda[layer], _lambda_init(layer))
        x2 = _ffn(x2, g_ffn2[layer], ffn2_w_gu[layer], ffn2_w_down[layer], mixed, w_out[layer],
                  g_final if layer == depth - 1 else None)
    return x2.reshape(b, s, d)
```

```python
import functools
import math

import numpy as np
import jax
import jax.numpy as jnp
from jax import lax
from jax.experimental import pallas as pl
from jax.experimental.pallas import tpu as pltpu

F32 = jnp.float32
BF16 = jnp.bfloat16

LANES = 128
HEAD_DIM = 64
HEADS_PER_GROUP = 4
GROUP_WIDTH = HEADS_PER_GROUP * HEAD_DIM
N_GROUPS = 4
CHUNK = 64
BAND_CHUNKS = 8
MAX_REL_DIST = 256
DIFF_QK_DIM = HEAD_DIM // 2
FFN_RES = 0.5
RMS_EPS = 1e-6
NEG = -1e30
LOG2E = math.log2(math.e)
N_SPLIT = 3
ONES_ROWS = 16

TOKEN_TILE = 512
ATTN_TILE = 256
CHAIN_GROUP = 2
MXU_DIM = 256
FFN_CHUNK = 6 * MXU_DIM
PREFIX_BLOCK = MXU_DIM
VMEM_LIMIT = 56 * 1024 * 1024

_NT = (((1,), (1,)), ((), ()))


def _rmsnorm(x, g):
    y = x * lax.rsqrt(jnp.mean(x * x, axis=-1, keepdims=True) + RMS_EPS)
    return y * g


def _log_sigmoid(x):
    return jnp.minimum(x, 0.0) - jnp.log1p(jnp.exp(-jnp.abs(x)))


def _split3(x):
    h1 = x.astype(BF16)
    r1 = x - h1.astype(F32)
    h2 = r1.astype(BF16)
    h3 = (r1 - h2.astype(F32)).astype(BF16)
    return h1, h2, h3


def _split3_const(c):
    parts = []
    for _ in range(N_SPLIT):
        p = float(np.asarray(c, np.float32).astype(jnp.bfloat16).astype(np.float32))
        parts.append(p)
        c = c - p
    return parts


def _params(sem):
    return pltpu.CompilerParams(dimension_semantics=sem, vmem_limit_bytes=VMEM_LIMIT)


def _resident(shape):
    zeros = (0,) * len(shape)
    return pl.BlockSpec(shape, lambda *_: zeros, pipeline_mode=pl.Buffered(1))


def _ffn_kernel(*refs, d_ff, chunk, n_mixed, final_norm):
    x_ref, refs = refs[0], refs[1:]
    x = x_ref[...]
    if n_mixed:
        wout_ref = refs[n_mixed]
        width = wout_ref.shape[0] // n_mixed
        for grp, m_ref in enumerate(refs[:n_mixed]):
            x = x + jnp.dot(m_ref[...], wout_ref[grp * width:(grp + 1) * width, :],
                            preferred_element_type=F32)
        refs = refs[n_mixed + 1:]
    g_ref, wgu_ref, wd_ref = refs[:3]
    o_ref = refs[-1]
    h = _rmsnorm(x, g_ref[...]).astype(BF16)
    acc = jnp.zeros(x.shape, F32)
    for c0 in range(0, d_ff, chunk):
        c1 = min(c0 + chunk, d_ff)
        gate = jnp.dot(h, wgu_ref[:, c0:c1], preferred_element_type=F32)
        up = jnp.dot(h, wgu_ref[:, d_ff + c0:d_ff + c1], preferred_element_type=F32)
        act = gate * jax.nn.sigmoid(gate) * up
        acc = acc + jnp.dot(act.astype(BF16), wd_ref[c0:c1, :], preferred_element_type=F32)
    y = x + FFN_RES * acc
    o_ref[...] = _rmsnorm(y, refs[3][...]) if final_norm else y


def _ffn(x2, g, w_gu, w_down, mixed=(), w_out=None, g_final=None):
    n, d = x2.shape
    d_ff = w_down.shape[0]
    chunk = FFN_CHUNK
    tm = min(TOKEN_TILE, n)
    row_spec = pl.BlockSpec((tm, d), lambda i: (i, 0))
    args, specs = [x2], [row_spec]
    if mixed:
        for m in mixed:
            args.append(m.reshape(n, m.shape[-1]))
            specs.append(pl.BlockSpec((tm, m.shape[-1]), lambda i: (i, 0)))
        args.append(w_out.astype(BF16))
        specs.append(_resident(w_out.shape))
    args += [g.reshape(1, d), w_gu.astype(BF16), w_down.astype(BF16)]
    specs += [_resident((1, d)), _resident((d, 2 * d_ff)), _resident((d_ff, d))]
    if g_final is not None:
        args.append(g_final.reshape(1, d))
        specs.append(_resident((1, d)))
    return pl.pallas_call(
        functools.partial(_ffn_kernel, d_ff=d_ff, chunk=chunk, n_mixed=len(mixed),
                          final_norm=g_final is not None),
        grid=(n // tm,),
        in_specs=specs,
        out_specs=row_spec,
        out_shape=jax.ShapeDtypeStruct((n, d), F32),
        compiler_params=_params(("arbitrary",)),
        name="ffn",
    )(*args)


def _inproj_kernel(x_ref, g_ref, wqk_ref, cs_ref, wvt_ref, wf_ref, bf_ref,
                   qk_ref, vt_ref, kaug_ref, qaug_ref, carry_ref, *, tm, blk):
    @pl.when(pl.program_id(1) == 0)
    def _():
        carry_ref[...] = jnp.zeros_like(carry_ref)

    h = _rmsnorm(x_ref[0], g_ref[...]).astype(BF16)
    qk = jnp.dot(h, wqk_ref[...], preferred_element_type=F32)
    qk_ref[0] = (qk * cs_ref[...]).astype(BF16)
    vt_ref[0] = lax.dot_general(wvt_ref[...], h, _NT, preferred_element_type=F32).astype(BF16)

    ri = lax.broadcasted_iota(jnp.int32, (blk, blk), 0)
    ci = lax.broadcasted_iota(jnp.int32, (blk, blk), 1)
    tril = jnp.where(ci <= ri, 1.0, 0.0).astype(BF16)
    ls = _log_sigmoid(jnp.dot(h, wf_ref[...], preferred_element_type=F32) + bf_ref[...])
    lane = lax.broadcasted_iota(jnp.int32, (1, LANES), 1)
    used = lane < 2 * N_SPLIT * HEADS_PER_GROUP
    slot = lane % (2 * N_SPLIT)
    carry = carry_ref[...]
    for r0 in range(0, tm, blk):
        cum = carry
        for part in _split3(ls[r0:r0 + blk]):
            cum = cum + jnp.dot(tril, part, preferred_element_type=F32)
        carry = cum[blk - 1:blk, :]
        kaug = jnp.where(slot >= N_SPLIT, 1.0, 0.0)
        qaug = jnp.where(slot < N_SPLIT, -1.0, 0.0)
        for i, part in enumerate(_split3(cum * LOG2E)):
            kaug = jnp.where(slot == i, part.astype(F32), kaug)
            qaug = jnp.where(slot == N_SPLIT + i, part.astype(F32), qaug)
        kaug_ref[0, r0:r0 + blk, :] = jnp.where(used, kaug, 0.0).astype(BF16)
        qaug_ref[0, r0:r0 + blk, :] = jnp.where(used, qaug, 0.0).astype(BF16)
    carry_ref[...] = carry


def _inproj(x3, g, w_in, b_f):
    b, s, d = x3.shape
    gw = GROUP_WIDTH
    nf = HEADS_PER_GROUP
    starts = (0, 3 * gw + nf, 6 * gw + nf, 9 * gw + nf)
    w_qk = jnp.concatenate([w_in[:, st:st + 2 * gw] for st in starts], axis=1).astype(BF16)
    w_vt = jnp.concatenate([w_in[:, st + 2 * gw:st + 3 * gw] for st in starts], axis=1).T.astype(BF16)
    rep = 2 * N_SPLIT
    w_f = jnp.pad(jnp.repeat(w_in[:, 3 * gw:3 * gw + nf], rep, axis=1),
                  ((0, 0), (0, LANES - rep * nf))).astype(BF16)
    b_col = jnp.pad(jnp.repeat(b_f, rep), (0, LANES - rep * nf)).reshape(1, LANES)
    width = w_qk.shape[1]
    cs = np.ones((1, width), np.float32)
    for grp in range(N_GROUPS):
        scale = DIFF_QK_DIM ** -0.5 if grp == 2 else HEAD_DIM ** -0.5
        cs[0, 2 * grp * gw:(2 * grp + 1) * gw] = scale * LOG2E
    tm = min(TOKEN_TILE, s)
    blk = min(PREFIX_BLOCK, tm)
    return pl.pallas_call(
        functools.partial(_inproj_kernel, tm=tm, blk=blk),
        grid=(b, s // tm),
        in_specs=[
            pl.BlockSpec((1, tm, d), lambda bi, j: (bi, j, 0)),
            _resident((1, d)),
            _resident((d, width)),
            _resident((1, width)),
            _resident((N_GROUPS * gw, d)),
            _resident((d, LANES)),
            _resident((1, LANES)),
        ],
        out_specs=[
            pl.BlockSpec((1, tm, width), lambda bi, j: (bi, j, 0)),
            pl.BlockSpec((1, N_GROUPS * gw, tm), lambda bi, j: (bi, 0, j)),
            pl.BlockSpec((1, tm, LANES), lambda bi, j: (bi, j, 0)),
            pl.BlockSpec((1, tm, LANES), lambda bi, j: (bi, j, 0)),
        ],
        out_shape=[
            jax.ShapeDtypeStruct((b, s, width), BF16),
            jax.ShapeDtypeStruct((b, N_GROUPS * gw, s), BF16),
            jax.ShapeDtypeStruct((b, s, LANES), BF16),
            jax.ShapeDtypeStruct((b, s, LANES), BF16),
        ],
        scratch_shapes=[pltpu.VMEM((1, LANES), F32)],
        compiler_params=_params(("arbitrary", "arbitrary")),
        name="inproj",
    )(x3, g.reshape(1, d), w_qk, jnp.asarray(cs), w_vt, w_f, b_col)


def _lane_range(lo, width):
    lane = lax.broadcasted_iota(jnp.int32, (1, LANES), 1)
    return (lane >= lo) & (lane < lo + width)


def _query_rows(ref, q_tile, t):
    return ref[0, pl.ds(pl.multiple_of(q_tile * t, t), t), :]


def _masked_queries(q, width):
    out = []
    for pair in range(2):
        q_pair = q[:, pair * LANES:(pair + 1) * LANES]
        for lo in range(0, LANES, width):
            out.append(jnp.where(_lane_range(lo, width), q_pair, jnp.zeros_like(q_pair)))
    return out


def _head_scores(k_ref, qs, t, per_head, k_extra=None):
    def scores(kb, cs):
        k0 = pl.multiple_of(kb * t, t)
        out = []
        for c in cs:
            pair = c // (2 * per_head)
            k = k_ref[0, pl.ds(k0, t), pair * LANES:(pair + 1) * LANES]
            if k_extra is not None:
                k = jnp.concatenate([k, k_extra(c, k0)], axis=1)
            out.append(lax.dot_general(k, qs[c], _NT, preferred_element_type=F32))
        return out
    return scores


def _head_products(vt_ref, t, per_head, with_ones=True):
    def products(ps, kb, cs):
        k0 = pl.multiple_of(kb * t, t)
        out = []
        for c, p in zip(cs, ps):
            h = c // per_head
            vt = vt_ref[0, h * HEAD_DIM:(h + 1) * HEAD_DIM, pl.ds(k0, t)]
            if with_ones:
                vt = jnp.concatenate([vt, jnp.ones((ONES_ROWS, t), vt.dtype)], axis=0)
            out.append(jnp.dot(vt, p, preferred_element_type=F32))
        return out
    return products


_MAX, _ALPHA, _OFFSET = range(3)


def _flash_forward(qi, n_q, t, n_chain, group, tile_ops, products, emit,
                   s_scr, p_scr, acc_scr, st_scr, trips_per_iter=1):
    chains = range(n_chain)
    kb_lo, kb_hi, scores, adjust, shift_of = tile_ops(qi)

    def land(kb, tiles, diag, cs=chains, adjust=adjust, shift_of=shift_of):
        slot = kb & 1
        for c, tile in zip(cs, tiles):
            s = adjust(c, tile, kb, diag)
            s_scr[slot, c] = s
            mx = jnp.max(s, axis=0, keepdims=True)
            shift = shift_of(c, kb, diag)
            if shift is not None:
                mx = mx + shift
            m_old = st_scr[_MAX, c]
            m_new = jnp.maximum(m_old, mx)
            st_scr[_MAX, c] = m_new
            st_scr[_ALPHA, c] = jnp.exp2(m_old - m_new)
            st_scr[_OFFSET, c] = m_new if shift is None else m_new - shift

    def weigh(kb, cs=chains):
        slot = kb & 1
        for c in cs:
            r = st_scr[_OFFSET, c]
            for lo in range(0, t, LANES):
                cols = slice(lo, lo + LANES)
                p_scr[c, :, cols] = jnp.exp2(s_scr[slot, c, :, cols] - r[:, cols]).astype(BF16)
        return products([p_scr[c] for c in cs], kb, cs)

    def step(kb, diag_next):
        for g0 in range(0, n_chain, group):
            cs = range(g0, min(g0 + group, n_chain))
            alpha = [st_scr[_ALPHA, c] for c in cs]
            nxt = scores(kb + 1, cs)
            prod = weigh(kb, cs)
            land(kb + 1, nxt, diag_next, cs)
            for c, a, pr in zip(cs, alpha, prod):
                acc_scr[c] = a * acc_scr[c] + pr

    def reset_stats():
        for c in chains:
            st_scr[_MAX, c] = jnp.full((1, t), NEG, F32)

    def finish():
        alpha = [st_scr[_ALPHA, c] for c in chains]
        prod = weigh(kb_hi)
        nxt_lo, _, nxt_scores, nxt_adjust, nxt_shift = tile_ops(jnp.minimum(qi + 1, n_q - 1))
        reset_stats()
        land(nxt_lo, nxt_scores(nxt_lo, chains), False, chains, nxt_adjust, nxt_shift)
        emit([alpha[c] * acc_scr[c] + prod[c] for c in chains])

    for c in chains:
        acc_scr[c] = jnp.zeros(acc_scr.shape[1:], F32)

    @pl.when(qi == 0)
    def _():
        reset_stats()
        land(kb_lo, scores(kb_lo, chains), True)
        finish()

    n_regular = jnp.maximum(kb_hi - 1 - kb_lo, 0)

    def body(i, carry):
        for u in range(trips_per_iter):
            step(kb_lo + trips_per_iter * i + u, False)
        return carry

    lax.fori_loop(0, n_regular // trips_per_iter, body, 0)
    for u in range(trips_per_iter - 1, 0, -1):

        @pl.when(n_regular % trips_per_iter >= u)
        def _(u=u):
            step(kb_hi - 1 - u, False)

    @pl.when(qi > 0)
    def _():
        step(kb_hi - 1, True)
        finish()


def _normalized(acc):
    return acc[:HEAD_DIM] * (1.0 / acc[HEAD_DIM:HEAD_DIM + 1])


def _store_pair(o_ref, q_tile, pair, top, bottom):
    o_t = jnp.concatenate([top, bottom], axis=0)
    t = o_t.shape[1]
    rows = pl.ds(pl.multiple_of(q_tile * t, t), t)
    o_ref[0, rows, pair * LANES:(pair + 1) * LANES] = o_t.T.astype(o_ref.dtype)


def _store_normalized(o_ref, q_tile, acc):
    for pair in range(2):
        _store_pair(o_ref, q_tile, pair, _normalized(acc[2 * pair]), _normalized(acc[2 * pair + 1]))


def _for_query_tiles(o_ref, t, tile_body):
    n_q = o_ref.shape[1] // t

    def body(qi, carry):
        tile_body(qi, n_q)
        return carry

    lax.fori_loop(0, n_q, body, 0)


def _tile_iotas(t):
    key = lax.broadcasted_iota(jnp.int32, (t, t), 0)
    qry = lax.broadcasted_iota(jnp.int32, (t, t), 1)
    return key, qry


def _no_shift(c, kb, diag):
    return None


def _fox_kernel(q_ref, k_ref, vt_ref, qaug_ref, kaug_ref, o_ref, s_scr, p_scr, acc_scr, st_scr,
                *, t):
    rep = 2 * N_SPLIT

    def k_extra(h, k0):
        return kaug_ref[0, pl.ds(k0, t), :]

    def adjust(h, s, kb, diag):
        if not diag:
            return s
        key, qry = _tile_iotas(t)
        return jnp.where(key <= qry, s, NEG)

    def tile_ops(q_tile):
        q_aug = _query_rows(qaug_ref, q_tile, t)
        qs = [jnp.concatenate(
                  [q, jnp.where(_lane_range(rep * h, rep), q_aug, jnp.zeros_like(q_aug))], axis=1)
              for h, q in enumerate(_masked_queries(_query_rows(q_ref, q_tile, t), HEAD_DIM))]
        return 0, q_tile, _head_scores(k_ref, qs, t, 1, k_extra), adjust, _no_shift

    def query_tile(qi, n_q):
        _flash_forward(qi, n_q, t, HEADS_PER_GROUP, 2 * CHAIN_GROUP,
                       tile_ops, _head_products(vt_ref, t, 1),
                       functools.partial(_store_normalized, o_ref, qi),
                       s_scr, p_scr, acc_scr, st_scr, trips_per_iter=2)

    _for_query_tiles(o_ref, t, query_tile)


def _band_kernel(q_ref, k_ref, vt_ref, bias_ref, o_ref, s_scr, p_scr, acc_scr, st_scr,
                 *, t, reach):
    def tile_ops(q_tile):
        qs = _masked_queries(_query_rows(q_ref, q_tile, t), HEAD_DIM)

        def adjust(h, s, kb, diag):
            return s + bias_ref[h, kb - q_tile + reach]

        return (jnp.maximum(q_tile - reach, 0), q_tile, _head_scores(k_ref, qs, t, 1), adjust,
                _no_shift)

    def query_tile(qi, n_q):
        _flash_forward(qi, n_q, t, HEADS_PER_GROUP, CHAIN_GROUP,
                       tile_ops, _head_products(vt_ref, t, 1),
                       functools.partial(_store_normalized, o_ref, qi),
                       s_scr, p_scr, acc_scr, st_scr, trips_per_iter=2)

    _for_query_tiles(o_ref, t, query_tile)


def _alibi_lanes(t):
    q_aug = np.zeros((HEADS_PER_GROUP, t, LANES), np.float32)
    k_aug = np.zeros((HEADS_PER_GROUP, t, LANES), np.float32)
    off = np.arange(t, dtype=np.float32)
    for h in range(HEADS_PER_GROUP):
        parts = _split3_const(_alibi_slope(h) * LOG2E)
        for i, c in enumerate(parts):
            q_aug[h, :, i] = off
            k_aug[h, :, i] = -c
            q_aug[h, :, N_SPLIT + i] = c
            k_aug[h, :, N_SPLIT + i] = off
    return jnp.asarray(q_aug, BF16), jnp.asarray(k_aug, BF16)


def _alibi_slope(h):
    return 2.0 ** (-8.0 * (h + 1) / HEADS_PER_GROUP)


def _alibi_diag_fix(t):
    key = np.arange(t)[:, None]
    qry = np.arange(t)[None, :]
    dist = np.minimum(qry - key, 0).astype(np.float32)
    fix = np.stack([dist * np.float32(2.0 * _alibi_slope(h) * LOG2E) for h in range(HEADS_PER_GROUP)])
    return jnp.asarray(np.where((key // CHUNK <= qry // CHUNK)[None], fix, np.float32(NEG)))


def _diff_kernel(q_ref, k_ref, vt_ref, lam_ref, qal_ref, kal_ref, fix_ref, o_ref,
                 s_scr, p_scr, acc_scr, st_scr, *, t, lam_init):
    lp = lam_ref[...]
    lam = (jnp.exp(jnp.sum(lp[0:1] * lp[1:2], axis=-1, keepdims=True))
           - jnp.exp(jnp.sum(lp[2:3] * lp[3:4], axis=-1, keepdims=True)) + lam_init)
    def k_extra(c, k0):
        return kal_ref[c // 2]

    def adjust(c, s, kb, diag):
        return s + fix_ref[c // 2] if diag else s

    def tile_ops(q_tile):
        qs = [jnp.concatenate([q, qal_ref[c // 2]], axis=1)
              for c, q in enumerate(_masked_queries(_query_rows(q_ref, q_tile, t), DIFF_QK_DIM))]

        def shift_of(c, kb, diag):
            if diag:
                return None
            return (-_alibi_slope(c // 2) * LOG2E * t) * (q_tile - kb).astype(F32)

        return 0, q_tile, _head_scores(k_ref, qs, t, 2, k_extra), adjust, shift_of

    n_chain = 2 * HEADS_PER_GROUP

    def emit(qi, acc):
        for pair in range(2):
            halves = []
            for hh in range(2):
                h = 2 * pair + hh
                o_h = _normalized(acc[2 * h]) - lam * _normalized(acc[2 * h + 1])
                ms = jnp.mean(o_h * o_h, axis=0, keepdims=True)
                halves.append(o_h * lax.rsqrt(ms + RMS_EPS) * (1.0 - lam_init))
            _store_pair(o_ref, qi, pair, halves[0], halves[1])

    def query_tile(qi, n_q):
        _flash_forward(qi, n_q, t, n_chain, 2 * CHAIN_GROUP, tile_ops,
                       _head_products(vt_ref, t, 2), functools.partial(emit, qi),
                       s_scr, p_scr, acc_scr, st_scr)

    _for_query_tiles(o_ref, t, query_tile)


def _softplus2(z):
    neg_abs = pltpu.bitcast(pltpu.bitcast(z, jnp.uint32) | jnp.uint32(0x80000000), F32)
    return jnp.maximum(z, 0.0) + jnp.log(1.0 + jnp.exp2(neg_abs)) * LOG2E


def _stick_kernel(q_ref, k_ref, vt_ref, o_ref, z_scr, w_scr, acc_scr, st_scr, wi_scr, *, t):
    _for_query_tiles(o_ref, t, lambda qi, n_q: _stick_tile(
        qi, q_ref, k_ref, vt_ref, o_ref, z_scr, w_scr, acc_scr, st_scr, wi_scr, t))


def _stick_tile(qi, q_ref, k_ref, vt_ref, o_ref, z_scr, w_scr, acc_scr, st_scr, wi_scr, t):
    chains = range(HEADS_PER_GROUP)
    products = _head_products(vt_ref, t, 1, with_ones=False)
    tail_scr = st_scr.at[0]

    scores = _head_scores(k_ref, _masked_queries(_query_rows(q_ref, qi, t), HEAD_DIM), t, 1)

    def land(j, tiles, diag, cs=chains):
        slot = j & 1
        key, qry = _tile_iotas(t)
        minus_suffix = jnp.where(qry >= key, -1.0, 0.0).astype(BF16)
        within = []
        for c, z in zip(cs, tiles):
            rest = _softplus2(z)
            if diag:
                rest = jnp.where(key < qry, rest, 0.0)
                z = jnp.where(key < qry, z, NEG)
            z_scr[slot, c] = z
            hi = rest.astype(BF16)
            lo = (rest - hi.astype(F32)).astype(BF16)
            within.append(jnp.dot(minus_suffix, hi, preferred_element_type=F32)
                          + jnp.dot(minus_suffix, lo, preferred_element_type=F32))
        return within

    def weigh(j, cs=chains):
        slot = j & 1
        for c in cs:
            tail = tail_scr[c]
            for lo in range(0, t, LANES):
                cols = slice(lo, lo + LANES)
                logw = z_scr[slot, c, :, cols] + wi_scr[c, :, cols] + tail[:, cols]
                w_scr[c, :, cols] = jnp.exp2(logw).astype(BF16)
            tail_scr[c] = tail + wi_scr[c, 0:1, :]
        return products([w_scr[c] for c in cs], qi - j, cs)

    def trip(j):
        nxt = scores(qi - j - 1, chains)
        prod = weigh(j)
        within = land(j + 1, nxt, False)
        for c in chains:
            acc_scr[c] = acc_scr[c] + prod[c]
            wi_scr[c] = within[c]

    def body(i, carry):
        trip(2 * i)
        trip(2 * i + 1)
        return carry

    for c, wi in enumerate(land(0, scores(qi, chains), True)):
        tail_scr[c] = jnp.zeros((1, t), F32)
        acc_scr[c] = jnp.zeros(acc_scr.shape[1:], F32)
        wi_scr[c] = wi
    lax.fori_loop(0, qi // 2, body, 0)

    def finish():
        prod = weigh(qi)
        acc = [acc_scr[c] + prod[c] for c in chains]
        for pair in range(2):
            _store_pair(o_ref, qi, pair, acc[2 * pair], acc[2 * pair + 1])

    @pl.when(qi % 2 == 1)
    def _():
        trip(qi - 1)
        finish()

    @pl.when(qi % 2 == 0)
    def _():
        finish()


def _mixer_call(body, qk, vt, group, n_chain, extra_inputs, extra_specs, name,
                acc_rows=HEAD_DIM + ONES_ROWS, extra_scratch=()):
    b, s, _ = qk.shape
    t = min(ATTN_TILE, s)
    gw = GROUP_WIDTH
    return pl.pallas_call(
        functools.partial(body, t=t),
        grid=(b,),
        in_specs=[
            pl.BlockSpec((1, s, gw), lambda bi: (bi, 0, 2 * group)),
            pl.BlockSpec((1, s, gw), lambda bi: (bi, 0, 2 * group + 1)),
            pl.BlockSpec((1, gw, s), lambda bi: (bi, group, 0)),
        ] + extra_specs,
        out_specs=pl.BlockSpec((1, s, gw), lambda bi: (bi, 0, 0)),
        out_shape=jax.ShapeDtypeStruct((b, s, gw), BF16),
        scratch_shapes=[pltpu.VMEM((2, n_chain, t, t), F32),
                        pltpu.VMEM((n_chain, t, t), BF16),
                        pltpu.VMEM((n_chain, acc_rows, t), F32),
                        pltpu.VMEM((3, n_chain, 1, t), F32)]
                       + list(extra_scratch),
        compiler_params=_params(("arbitrary",)),
        name=name,
    )(qk, qk, vt, *extra_inputs)


def _band_bias(rel_table, t, reach):
    n_keys = (reach + 1) * t
    period = n_keys + t
    d = np.arange(period)
    d = np.where(d < t, d, d - period)
    idx = np.clip(reach * t + d, -MAX_REL_DIST, MAX_REL_DIST) + MAX_REL_DIST
    ext = rel_table.astype(F32)[:, idx] * LOG2E
    h = rel_table.shape[0]
    rows = jnp.tile(ext, (1, n_keys))[:, :n_keys * (period - 1)].reshape(h, n_keys, period - 1)
    bias = rows[:, :, :t]
    kc = (np.arange(n_keys) // CHUNK - (reach * t // CHUNK - BAND_CHUNKS))[:, None]
    qc = (np.arange(t) // CHUNK)[None, :]
    visible = (kc >= qc) & (kc <= qc + BAND_CHUNKS)
    return jnp.where(visible[None], bias, NEG).reshape(h, reach + 1, t, t)


def _mixers(qk, vt, kaug, qaug, rel_table, lam_params, lam_init):
    b, s, _ = qk.shape
    t = min(ATTN_TILE, s)
    assert s % t == 0 and t % CHUNK == 0 and t % LANES == 0
    reach = -(-BAND_CHUNKS * CHUNK // t)
    nh = HEADS_PER_GROUP
    o_a = _mixer_call(
        _fox_kernel, qk, vt, 0, nh, [qaug, kaug],
        [pl.BlockSpec((1, s, LANES), lambda bi: (bi, 0, 0)),
         pl.BlockSpec((1, s, LANES), lambda bi: (bi, 0, 0))], "fox")
    bias = _band_bias(rel_table, t, reach)
    o_b = _mixer_call(
        functools.partial(_band_kernel, reach=reach), qk, vt, 1, nh, [bias],
        [_resident(bias.shape)], "band")
    q_al, k_al = _alibi_lanes(t)
    fix = _alibi_diag_fix(t)
    o_c = _mixer_call(
        functools.partial(_diff_kernel, lam_init=lam_init), qk, vt, 2, 2 * nh,
        [lam_params, q_al, k_al, fix],
        [_resident(lam_params.shape), _resident(q_al.shape), _resident(k_al.shape),
         _resident(fix.shape)], "diff")
    o_d = _mixer_call(_stick_kernel, qk, vt, 3, nh, [], [], "stick", acc_rows=HEAD_DIM,
                      extra_scratch=[pltpu.VMEM((nh, t, t), F32)])
    return o_a, o_b, o_c, o_d


def _lambda_init(layer_idx):
    return 0.8 - 0.6 * math.exp(-0.3 * layer_idx)


def kernel(x, g_ffn1, ffn1_w_gu, ffn1_w_down, g_mix, w_in, b_f, rel_bias, diff_lambda, w_out,
           g_ffn2, ffn2_w_gu, ffn2_w_down, g_final):
    b, s, d = x.shape
    depth = g_ffn1.shape[0]
    x2 = x.reshape(b * s, d)
    for layer in range(depth):
        x2 = _ffn(x2, g_ffn1[layer], ffn1_w_gu[layer], ffn1_w_down[layer])
        qk, vt, kaug, qaug = _inproj(x2.reshape(b, s, d), g_mix[layer], w_in[layer], b_f[layer])
        mixed = _mixers(qk, vt, kaug, qaug, rel_bias[layer], diff_lambda[layer], _lambda_init(layer))
        x2 = _ffn(x2, g_ffn2[layer], ffn2_w_gu[layer], ffn2_w_down[layer], mixed, w_out[layer],
                  g_final if layer == depth - 1 else None)
    return x2.reshape(b, s, d)
```

```python
import functools
import math

import numpy as np
import jax
import jax.numpy as jnp
from jax import lax
from jax.experimental import pallas as pl
from jax.experimental.pallas import tpu as pltpu

F32 = jnp.float32
BF16 = jnp.bfloat16

LANES = 128
HEAD_DIM = 64
HEADS_PER_GROUP = 4
GROUP_WIDTH = HEADS_PER_GROUP * HEAD_DIM
N_GROUPS = 4
CHUNK = 64
BAND_CHUNKS = 8
MAX_REL_DIST = 256
DIFF_QK_DIM = HEAD_DIM // 2
FFN_RES = 0.5
RMS_EPS = 1e-6
NEG = -1e30
LOG2E = math.log2(math.e)
N_SPLIT = 3
ONES_ROWS = 16

TOKEN_TILE = 512
ATTN_TILE = 256
CHAIN_GROUP = 2
MXU_DIM = 256
FFN_CHUNK = 6 * MXU_DIM
PREFIX_BLOCK = MXU_DIM
VMEM_LIMIT = 56 * 1024 * 1024

_NT = (((1,), (1,)), ((), ()))


def _rmsnorm(x, g):
    y = x * lax.rsqrt(jnp.mean(x * x, axis=-1, keepdims=True) + RMS_EPS)
    return y * g


def _log_sigmoid(x):
    return jnp.minimum(x, 0.0) - jnp.log1p(jnp.exp(-jnp.abs(x)))


def _split3(x):
    h1 = x.astype(BF16)
    r1 = x - h1.astype(F32)
    h2 = r1.astype(BF16)
    h3 = (r1 - h2.astype(F32)).astype(BF16)
    return h1, h2, h3


def _split3_const(c):
    parts = []
    for _ in range(N_SPLIT):
        p = float(np.asarray(c, np.float32).astype(jnp.bfloat16).astype(np.float32))
        parts.append(p)
        c = c - p
    return parts


def _params(sem):
    return pltpu.CompilerParams(dimension_semantics=sem, vmem_limit_bytes=VMEM_LIMIT)


def _resident(shape):
    zeros = (0,) * len(shape)
    return pl.BlockSpec(shape, lambda *_: zeros, pipeline_mode=pl.Buffered(1))


def _ffn_kernel(*refs, d_ff, chunk, n_mixed, final_norm):
    x_ref, refs = refs[0], refs[1:]
    x = x_ref[...]
    if n_mixed:
        wout_ref = refs[n_mixed]
        width = wout_ref.shape[0] // n_mixed
        for grp, m_ref in enumerate(refs[:n_mixed]):
            x = x + jnp.dot(m_ref[...], wout_ref[grp * width:(grp + 1) * width, :],
                            preferred_element_type=F32)
        refs = refs[n_mixed + 1:]
    g_ref, wgu_ref, wd_ref = refs[:3]
    o_ref = refs[-1]
    h = _rmsnorm(x, g_ref[...]).astype(BF16)
    acc = jnp.zeros(x.shape, F32)
    for c0 in range(0, d_ff, chunk):
        c1 = min(c0 + chunk, d_ff)
        gate = jnp.dot(h, wgu_ref[:, c0:c1], preferred_element_type=F32)
        up = jnp.dot(h, wgu_ref[:, d_ff + c0:d_ff + c1], preferred_element_type=F32)
        act = gate * jax.nn.sigmoid(gate) * up
        acc = acc + jnp.dot(act.astype(BF16), wd_ref[c0:c1, :], preferred_element_type=F32)
    y = x + FFN_RES * acc
    o_ref[...] = _rmsnorm(y, refs[3][...]) if final_norm else y


def _ffn(x2, g, w_gu, w_down, mixed=(), w_out=None, g_final=None):
    n, d = x2.shape
    d_ff = w_down.shape[0]
    chunk = FFN_CHUNK
    tm = min(TOKEN_TILE, n)
    row_spec = pl.BlockSpec((tm, d), lambda i: (i, 0))
    args, specs = [x2], [row_spec]
    if mixed:
        for m in mixed:
            args.append(m.reshape(n, m.shape[-1]))
            specs.append(pl.BlockSpec((tm, m.shape[-1]), lambda i: (i, 0)))
        args.append(w_out.astype(BF16))
        specs.append(_resident(w_out.shape))
    args += [g.reshape(1, d), w_gu.astype(BF16), w_down.astype(BF16)]
    specs += [_resident((1, d)), _resident((d, 2 * d_ff)), _resident((d_ff, d))]
    if g_final is not None:
        args.append(g_final.reshape(1, d))
        specs.append(_resident((1, d)))
    return pl.pallas_call(
        functools.partial(_ffn_kernel, d_ff=d_ff, chunk=chunk, n_mixed=len(mixed),
                          final_norm=g_final is not None),
        grid=(n // tm,),
        in_specs=specs,
        out_specs=row_spec,
        out_shape=jax.ShapeDtypeStruct((n, d), F32),
        compiler_params=_params(("arbitrary",)),
        name="ffn",
    )(*args)


def _inproj_kernel(x_ref, g_ref, wqk_ref, cs_ref, wvt_ref, wf_ref, bf_ref,
                   qk_ref, vt_ref, kaug_ref, qaug_ref, carry_ref, *, tm, blk):
    @pl.when(pl.program_id(1) == 0)
    def _():
        carry_ref[...] = jnp.zeros_like(carry_ref)

    h = _rmsnorm(x_ref[0], g_ref[...]).astype(BF16)
    qk = jnp.dot(h, wqk_ref[...], preferred_element_type=F32)
    qk_ref[0] = (qk * cs_ref[...]).astype(BF16)
    vt_ref[0] = lax.dot_general(wvt_ref[...], h, _NT, preferred_element_type=F32).astype(BF16)

    ri = lax.broadcasted_iota(jnp.int32, (blk, blk), 0)
    ci = lax.broadcasted_iota(jnp.int32, (blk, blk), 1)
    tril = jnp.where(ci <= ri, 1.0, 0.0).astype(BF16)
    ls = _log_sigmoid(jnp.dot(h, wf_ref[...], preferred_element_type=F32) + bf_ref[...])
    lane = lax.broadcasted_iota(jnp.int32, (1, LANES), 1)
    used = lane < 2 * N_SPLIT * HEADS_PER_GROUP
    slot = lane % (2 * N_SPLIT)
    carry = carry_ref[...]
    for r0 in range(0, tm, blk):
        cum = carry
        for part in _split3(ls[r0:r0 + blk]):
            cum = cum + jnp.dot(tril, part, preferred_element_type=F32)
        carry = cum[blk - 1:blk, :]
        kaug = jnp.where(slot >= N_SPLIT, 1.0, 0.0)
        qaug = jnp.where(slot < N_SPLIT, -1.0, 0.0)
        for i, part in enumerate(_split3(cum * LOG2E)):
            kaug = jnp.where(slot == i, part.astype(F32), kaug)
            qaug = jnp.where(slot == N_SPLIT + i, part.astype(F32), qaug)
        kaug_ref[0, r0:r0 + blk, :] = jnp.where(used, kaug, 0.0).astype(BF16)
        qaug_ref[0, r0:r0 + blk, :] = jnp.where(used, qaug, 0.0).astype(BF16)
    carry_ref[...] = carry


def _inproj(x3, g, w_in, b_f):
    b, s, d = x3.shape
    gw = GROUP_WIDTH
    nf = HEADS_PER_GROUP
    starts = (0, 3 * gw + nf, 6 * gw + nf, 9 * gw + nf)
    w_qk = jnp.concatenate([w_in[:, st:st + 2 * gw] for st in starts], axis=1).astype(BF16)
    w_vt = jnp.concatenate([w_in[:, st + 2 * gw:st + 3 * gw] for st in starts], axis=1).T.astype(BF16)
    rep = 2 * N_SPLIT
    w_f = jnp.pad(jnp.repeat(w_in[:, 3 * gw:3 * gw + nf], rep, axis=1),
                  ((0, 0), (0, LANES - rep * nf))).astype(BF16)
    b_col = jnp.pad(jnp.repeat(b_f, rep), (0, LANES - rep * nf)).reshape(1, LANES)
    width = w_qk.shape[1]
    cs = np.ones((1, width), np.float32)
    for grp in range(N_GROUPS):
        scale = DIFF_QK_DIM ** -0.5 if grp == 2 else HEAD_DIM ** -0.5
        cs[0, 2 * grp * gw:(2 * grp + 1) * gw] = scale * LOG2E
    tm = min(TOKEN_TILE, s)
    blk = min(PREFIX_BLOCK, tm)
    return pl.pallas_call(
        functools.partial(_inproj_kernel, tm=tm, blk=blk),
        grid=(b, s // tm),
        in_specs=[
            pl.BlockSpec((1, tm, d), lambda bi, j: (bi, j, 0)),
            _resident((1, d)),
            _resident((d, width)),
            _resident((1, width)),
            _resident((N_GROUPS * gw, d)),
            _resident((d, LANES)),
            _resident((1, LANES)),
        ],
        out_specs=[
            pl.BlockSpec((1, tm, width), lambda bi, j: (bi, j, 0)),
            pl.BlockSpec((1, N_GROUPS * gw, tm), lambda bi, j: (bi, 0, j)),
            pl.BlockSpec((1, tm, LANES), lambda bi, j: (bi, j, 0)),
            pl.BlockSpec((1, tm, LANES), lambda bi, j: (bi, j, 0)),
        ],
        out_shape=[
            jax.ShapeDtypeStruct((b, s, width), BF16),
            jax.ShapeDtypeStruct((b, N_GROUPS * gw, s), BF16),
            jax.ShapeDtypeStruct((b, s, LANES), BF16),
            jax.ShapeDtypeStruct((b, s, LANES), BF16),
        ],
        scratch_shapes=[pltpu.VMEM((1, LANES), F32)],
        compiler_params=_params(("arbitrary", "arbitrary")),
        name="inproj",
    )(x3, g.reshape(1, d), w_qk, jnp.asarray(cs), w_vt, w_f, b_col)


def _lane_range(lo, width):
    lane = lax.broadcasted_iota(jnp.int32, (1, LANES), 1)
    return (lane >= lo) & (lane < lo + width)


def _tile_start(tile, t):
    return tile * t if isinstance(tile, int) else pl.multiple_of(tile * t, t)


def _query_rows(ref, q_tile, t):
    return ref[0, pl.ds(_tile_start(q_tile, t), t), :]


def _masked_queries(q, width):
    out = []
    for pair in range(2):
        q_pair = q[:, pair * LANES:(pair + 1) * LANES]
        for lo in range(0, LANES, width):
            out.append(jnp.where(_lane_range(lo, width), q_pair, jnp.zeros_like(q_pair)))
    return out


def _head_scores(k_ref, qs, t, per_head, k_extra=None):
    def scores(kb, cs):
        k0 = _tile_start(kb, t)
        out = []
        for c in cs:
            pair = c // (2 * per_head)
            k = k_ref[0, pl.ds(k0, t), pair * LANES:(pair + 1) * LANES]
            if k_extra is not None:
                k = jnp.concatenate([k, k_extra(c, k0)], axis=1)
            out.append(lax.dot_general(k, qs[c], _NT, preferred_element_type=F32))
        return out
    return scores


def _head_products(vt_ref, t, per_head, with_ones=True):
    def products(ps, kb, cs):
        k0 = _tile_start(kb, t)
        out = []
        for c, p in zip(cs, ps):
            h = c // per_head
            vt = vt_ref[0, h * HEAD_DIM:(h + 1) * HEAD_DIM, pl.ds(k0, t)]
            if with_ones:
                vt = jnp.concatenate([vt, jnp.ones((ONES_ROWS, t), vt.dtype)], axis=0)
            out.append(jnp.dot(vt, p, preferred_element_type=F32))
        return out
    return products


_MAX, _ALPHA, _OFFSET = range(3)


def _flash_forward(qi, n_q, t, n_chain, group, tile_ops, products, emit,
                   s_scr, p_scr, acc_scr, st_scr, trips_per_iter=1):
    chains = range(n_chain)
    kb_lo, kb_hi, scores, adjust, shift_of = tile_ops(qi)

    def land(kb, tiles, diag, cs=chains, adjust=adjust, shift_of=shift_of):
        slot = kb & 1
        for c, tile in zip(cs, tiles):
            s = adjust(c, tile, kb, diag)
            s_scr[slot, c] = s
            mx = jnp.max(s, axis=0, keepdims=True)
            shift = shift_of(c, kb, diag)
            if shift is not None:
                mx = mx + shift
            m_old = st_scr[_MAX, c]
            m_new = jnp.maximum(m_old, mx)
            st_scr[_MAX, c] = m_new
            st_scr[_ALPHA, c] = jnp.exp2(m_old - m_new)
            st_scr[_OFFSET, c] = m_new if shift is None else m_new - shift

    def weigh(kb, cs=chains):
        slot = kb & 1
        for c in cs:
            r = st_scr[_OFFSET, c]
            for lo in range(0, t, LANES):
                cols = slice(lo, lo + LANES)
                p_scr[c, :, cols] = jnp.exp2(s_scr[slot, c, :, cols] - r[:, cols]).astype(BF16)
        return products([p_scr[c] for c in cs], kb, cs)

    def step(kb, diag_next):
        for g0 in range(0, n_chain, group):
            cs = range(g0, min(g0 + group, n_chain))
            alpha = [st_scr[_ALPHA, c] for c in cs]
            nxt = scores(kb + 1, cs)
            prod = weigh(kb, cs)
            land(kb + 1, nxt, diag_next, cs)
            for c, a, pr in zip(cs, alpha, prod):
                acc_scr[c] = a * acc_scr[c] + pr

    def reset_stats():
        for c in chains:
            st_scr[_MAX, c] = jnp.full((1, t), NEG, F32)

    def finish():
        alpha = [st_scr[_ALPHA, c] for c in chains]
        prod = weigh(kb_hi)
        if not (isinstance(qi, int) and qi + 1 == n_q):
            nxt = qi + 1 if isinstance(qi, int) else jnp.minimum(qi + 1, n_q - 1)
            nxt_lo, _, nxt_scores, nxt_adjust, nxt_shift = tile_ops(nxt)
            reset_stats()
            land(nxt_lo, nxt_scores(nxt_lo, chains), False, chains, nxt_adjust, nxt_shift)
        emit([alpha[c] * acc_scr[c] + prod[c] for c in chains])

    for c in chains:
        acc_scr[c] = jnp.zeros(acc_scr.shape[1:], F32)

    if isinstance(qi, int):
        if qi == 0:
            reset_stats()
            land(kb_lo, scores(kb_lo, chains), True)
        for kb in range(kb_lo, kb_hi):
            step(kb, kb == kb_hi - 1)
        finish()
        return

    @pl.when(qi == 0)
    def _():
        reset_stats()
        land(kb_lo, scores(kb_lo, chains), True)
        finish()

    n_regular = jnp.maximum(kb_hi - 1 - kb_lo, 0)

    def body(i, carry):
        for u in range(trips_per_iter):
            step(kb_lo + trips_per_iter * i + u, False)
        return carry

    lax.fori_loop(0, n_regular // trips_per_iter, body, 0)
    for u in range(trips_per_iter - 1, 0, -1):

        @pl.when(n_regular % trips_per_iter >= u)
        def _(u=u):
            step(kb_hi - 1 - u, False)

    @pl.when(qi > 0)
    def _():
        step(kb_hi - 1, True)
        finish()


def _normalized(acc):
    return acc[:HEAD_DIM] * (1.0 / acc[HEAD_DIM:HEAD_DIM + 1])


def _store_pair(o_ref, q_tile, pair, top, bottom):
    o_t = jnp.concatenate([top, bottom], axis=0)
    t = o_t.shape[1]
    rows = pl.ds(_tile_start(q_tile, t), t)
    o_ref[0, rows, pair * LANES:(pair + 1) * LANES] = o_t.T.astype(o_ref.dtype)


def _store_normalized(o_ref, q_tile, acc):
    for pair in range(2):
        _store_pair(o_ref, q_tile, pair, _normalized(acc[2 * pair]), _normalized(acc[2 * pair + 1]))


def _for_query_tiles(o_ref, t, tile_body, unroll=False):
    n_q = o_ref.shape[1] // t
    if unroll:
        for qi in range(n_q):
            tile_body(qi, n_q)
        return

    def body(qi, carry):
        tile_body(qi, n_q)
        return carry

    lax.fori_loop(0, n_q, body, 0)


def _tile_iotas(t):
    key = lax.broadcasted_iota(jnp.int32, (t, t), 0)
    qry = lax.broadcasted_iota(jnp.int32, (t, t), 1)
    return key, qry


def _no_shift(c, kb, diag):
    return None


def _fox_kernel(q_ref, k_ref, vt_ref, qaug_ref, kaug_ref, o_ref, s_scr, p_scr, acc_scr, st_scr,
                *, t):
    rep = 2 * N_SPLIT

    def k_extra(h, k0):
        return kaug_ref[0, pl.ds(k0, t), :]

    def adjust(h, s, kb, diag):
        if not diag:
            return s
        key, qry = _tile_iotas(t)
        return jnp.where(key <= qry, s, NEG)

    def tile_ops(q_tile):
        q_aug = _query_rows(qaug_ref, q_tile, t)
        qs = [jnp.concatenate(
                  [q, jnp.where(_lane_range(rep * h, rep), q_aug, jnp.zeros_like(q_aug))], axis=1)
              for h, q in enumerate(_masked_queries(_query_rows(q_ref, q_tile, t), HEAD_DIM))]
        return 0, q_tile, _head_scores(k_ref, qs, t, 1, k_extra), adjust, _no_shift

    def query_tile(qi, n_q):
        _flash_forward(qi, n_q, t, HEADS_PER_GROUP, 2 * CHAIN_GROUP,
                       tile_ops, _head_products(vt_ref, t, 1),
                       functools.partial(_store_normalized, o_ref, qi),
                       s_scr, p_scr, acc_scr, st_scr, trips_per_iter=2)

    _for_query_tiles(o_ref, t, query_tile, unroll=True)


def _band_kernel(q_ref, k_ref, vt_ref, bias_ref, o_ref, s_scr, p_scr, acc_scr, st_scr,
                 *, t, reach):
    def tile_ops(q_tile):
        qs = _masked_queries(_query_rows(q_ref, q_tile, t), HEAD_DIM)

        def adjust(h, s, kb, diag):
            return s + bias_ref[h, kb - q_tile + reach]

        first = q_tile - reach
        first = max(first, 0) if isinstance(first, int) else jnp.maximum(first, 0)
        return first, q_tile, _head_scores(k_ref, qs, t, 1), adjust, _no_shift

    def query_tile(qi, n_q):
        _flash_forward(qi, n_q, t, HEADS_PER_GROUP, CHAIN_GROUP,
                       tile_ops, _head_products(vt_ref, t, 1),
                       functools.partial(_store_normalized, o_ref, qi),
                       s_scr, p_scr, acc_scr, st_scr, trips_per_iter=2)

    _for_query_tiles(o_ref, t, query_tile, unroll=True)


def _alibi_lanes(t):
    q_aug = np.zeros((HEADS_PER_GROUP, t, LANES), np.float32)
    k_aug = np.zeros((HEADS_PER_GROUP, t, LANES), np.float32)
    off = np.arange(t, dtype=np.float32)
    for h in range(HEADS_PER_GROUP):
        parts = _split3_const(_alibi_slope(h) * LOG2E)
        for i, c in enumerate(parts):
            q_aug[h, :, i] = off
            k_aug[h, :, i] = -c
            q_aug[h, :, N_SPLIT + i] = c
            k_aug[h, :, N_SPLIT + i] = off
    return jnp.asarray(q_aug, BF16), jnp.asarray(k_aug, BF16)


def _alibi_slope(h):
    return 2.0 ** (-8.0 * (h + 1) / HEADS_PER_GROUP)


def _alibi_diag_fix(t):
    key = np.arange(t)[:, None]
    qry = np.arange(t)[None, :]
    dist = np.minimum(qry - key, 0).astype(np.float32)
    fix = np.stack([dist * np.float32(2.0 * _alibi_slope(h) * LOG2E) for h in range(HEADS_PER_GROUP)])
    return jnp.asarray(np.where((key // CHUNK <= qry // CHUNK)[None], fix, np.float32(NEG)))


def _diff_kernel(q_ref, k_ref, vt_ref, lam_ref, qal_ref, kal_ref, fix_ref, o_ref,
                 s_scr, p_scr, acc_scr, st_scr, *, t, lam_init):
    lp = lam_ref[...]
    lam = (jnp.exp(jnp.sum(lp[0:1] * lp[1:2], axis=-1, keepdims=True))
           - jnp.exp(jnp.sum(lp[2:3] * lp[3:4], axis=-1, keepdims=True)) + lam_init)
    def k_extra(c, k0):
        return kal_ref[c // 2]

    def adjust(c, s, kb, diag):
        return s + fix_ref[c // 2] if diag else s

    def tile_ops(q_tile):
        qs = [jnp.concatenate([q, qal_ref[c // 2]], axis=1)
              for c, q in enumerate(_masked_queries(_query_rows(q_ref, q_tile, t), DIFF_QK_DIM))]

        def shift_of(c, kb, diag):
            if diag:
                return None
            return -_alibi_slope(c // 2) * LOG2E * t * (q_tile - kb)

        return 0, q_tile, _head_scores(k_ref, qs, t, 2, k_extra), adjust, shift_of

    n_chain = 2 * HEADS_PER_GROUP

    def emit(qi, acc):
        for pair in range(2):
            halves = []
            for hh in range(2):
                h = 2 * pair + hh
                o_h = _normalized(acc[2 * h]) - lam * _normalized(acc[2 * h + 1])
                ms = jnp.mean(o_h * o_h, axis=0, keepdims=True)
                halves.append(o_h * lax.rsqrt(ms + RMS_EPS) * (1.0 - lam_init))
            _store_pair(o_ref, qi, pair, halves[0], halves[1])

    def query_tile(qi, n_q):
        _flash_forward(qi, n_q, t, n_chain, 2 * CHAIN_GROUP, tile_ops,
                       _head_products(vt_ref, t, 2), functools.partial(emit, qi),
                       s_scr, p_scr, acc_scr, st_scr)

    _for_query_tiles(o_ref, t, query_tile, unroll=True)


def _softplus2(z):
    neg_abs = pltpu.bitcast(pltpu.bitcast(z, jnp.uint32) | jnp.uint32(0x80000000), F32)
    return jnp.maximum(z, 0.0) + jnp.log(1.0 + jnp.exp2(neg_abs)) * LOG2E


def _stick_kernel(q_ref, k_ref, vt_ref, o_ref, z_scr, w_scr, acc_scr, st_scr, wi_scr, *, t):
    _for_query_tiles(o_ref, t, lambda qi, n_q: _stick_tile(
        qi, q_ref, k_ref, vt_ref, o_ref, z_scr, w_scr, acc_scr, st_scr, wi_scr, t), unroll=True)


def _stick_tile(qi, q_ref, k_ref, vt_ref, o_ref, z_scr, w_scr, acc_scr, st_scr, wi_scr, t):
    chains = range(HEADS_PER_GROUP)
    products = _head_products(vt_ref, t, 1, with_ones=False)
    tail_scr = st_scr.at[0]

    scores = _head_scores(k_ref, _masked_queries(_query_rows(q_ref, qi, t), HEAD_DIM), t, 1)

    def land(j, tiles, diag, cs=chains):
        slot = j & 1
        key, qry = _tile_iotas(t)
        minus_suffix = jnp.where(qry >= key, -1.0, 0.0).astype(BF16)
        within = []
        for c, z in zip(cs, tiles):
            rest = _softplus2(z)
            if diag:
                rest = jnp.where(key < qry, rest, 0.0)
                z = jnp.where(key < qry, z, NEG)
            z_scr[slot, c] = z
            hi = rest.astype(BF16)
            lo = (rest - hi.astype(F32)).astype(BF16)
            within.append(jnp.dot(minus_suffix, hi, preferred_element_type=F32)
                          + jnp.dot(minus_suffix, lo, preferred_element_type=F32))
        return within

    def weigh(j, cs=chains):
        slot = j & 1
        for c in cs:
            tail = tail_scr[c]
            for lo in range(0, t, LANES):
                cols = slice(lo, lo + LANES)
                logw = z_scr[slot, c, :, cols] + wi_scr[c, :, cols] + tail[:, cols]
                w_scr[c, :, cols] = jnp.exp2(logw).astype(BF16)
            tail_scr[c] = tail + wi_scr[c, 0:1, :]
        return products([w_scr[c] for c in cs], qi - j, cs)

    def trip(j):
        nxt = scores(qi - j - 1, chains)
        prod = weigh(j)
        within = land(j + 1, nxt, False)
        for c in chains:
            acc_scr[c] = acc_scr[c] + prod[c]
            wi_scr[c] = within[c]

    def body(i, carry):
        trip(2 * i)
        trip(2 * i + 1)
        return carry

    for c, wi in enumerate(land(0, scores(qi, chains), True)):
        tail_scr[c] = jnp.zeros((1, t), F32)
        acc_scr[c] = jnp.zeros(acc_scr.shape[1:], F32)
        wi_scr[c] = wi

    def finish():
        prod = weigh(qi)
        acc = [acc_scr[c] + prod[c] for c in chains]
        for pair in range(2):
            _store_pair(o_ref, qi, pair, acc[2 * pair], acc[2 * pair + 1])

    if isinstance(qi, int):
        for j in range(qi):
            trip(j)
        finish()
        return

    lax.fori_loop(0, qi // 2, body, 0)

    @pl.when(qi % 2 == 1)
    def _():
        trip(qi - 1)
        finish()

    @pl.when(qi % 2 == 0)
    def _():
        finish()


def _mixer_call(body, qk, vt, group, n_chain, extra_inputs, extra_specs, name,
                acc_rows=HEAD_DIM + ONES_ROWS, extra_scratch=()):
    b, s, _ = qk.shape
    t = min(ATTN_TILE, s)
    gw = GROUP_WIDTH
    return pl.pallas_call(
        functools.partial(body, t=t),
        grid=(b,),
        in_specs=[
            pl.BlockSpec((1, s, gw), lambda bi: (bi, 0, 2 * group)),
            pl.BlockSpec((1, s, gw), lambda bi: (bi, 0, 2 * group + 1)),
            pl.BlockSpec((1, gw, s), lambda bi: (bi, group, 0)),
        ] + extra_specs,
        out_specs=pl.BlockSpec((1, s, gw), lambda bi: (bi, 0, 0)),
        out_shape=jax.ShapeDtypeStruct((b, s, gw), BF16),
        scratch_shapes=[pltpu.VMEM((2, n_chain, t, t), F32),
                        pltpu.VMEM((n_chain, t, t), BF16),
                        pltpu.VMEM((n_chain, acc_rows, t), F32),
                        pltpu.VMEM((3, n_chain, 1, t), F32)]
                       + list(extra_scratch),
        compiler_params=_params(("arbitrary",)),
        name=name,
    )(qk, qk, vt, *extra_inputs)


def _band_bias(rel_table, t, reach):
    n_keys = (reach + 1) * t
    period = n_keys + t
    d = np.arange(period)
    d = np.where(d < t, d, d - period)
    idx = np.clip(reach * t + d, -MAX_REL_DIST, MAX_REL_DIST) + MAX_REL_DIST
    ext = rel_table.astype(F32)[:, idx] * LOG2E
    h = rel_table.shape[0]
    rows = jnp.tile(ext, (1, n_keys))[:, :n_keys * (period - 1)].reshape(h, n_keys, period - 1)
    bias = rows[:, :, :t]
    kc = (np.arange(n_keys) // CHUNK - (reach * t // CHUNK - BAND_CHUNKS))[:, None]
    qc = (np.arange(t) // CHUNK)[None, :]
    visible = (kc >= qc) & (kc <= qc + BAND_CHUNKS)
    return jnp.where(visible[None], bias, NEG).reshape(h, reach + 1, t, t)


def _mixers(qk, vt, kaug, qaug, rel_table, lam_params, lam_init):
    b, s, _ = qk.shape
    t = min(ATTN_TILE, s)
    assert s % t == 0 and t % CHUNK == 0 and t % LANES == 0
    reach = -(-BAND_CHUNKS * CHUNK // t)
    nh = HEADS_PER_GROUP
    o_a = _mixer_call(
        _fox_kernel, qk, vt, 0, nh, [qaug, kaug],
        [pl.BlockSpec((1, s, LANES), lambda bi: (bi, 0, 0)),
         pl.BlockSpec((1, s, LANES), lambda bi: (bi, 0, 0))], "fox")
    bias = _band_bias(rel_table, t, reach)
    o_b = _mixer_call(
        functools.partial(_band_kernel, reach=reach), qk, vt, 1, nh, [bias],
        [_resident(bias.shape)], "band")
    q_al, k_al = _alibi_lanes(t)
    fix = _alibi_diag_fix(t)
    o_c = _mixer_call(
        functools.partial(_diff_kernel, lam_init=lam_init), qk, vt, 2, 2 * nh,
        [lam_params, q_al, k_al, fix],
        [_resident(lam_params.shape), _resident(q_al.shape), _resident(k_al.shape),
         _resident(fix.shape)], "diff")
    o_d = _mixer_call(_stick_kernel, qk, vt, 3, nh, [], [], "stick", acc_rows=HEAD_DIM,
                      extra_scratch=[pltpu.VMEM((nh, t, t), F32)])
    return o_a, o_b, o_c, o_d


def _lambda_init(layer_idx):
    return 0.8 - 0.6 * math.exp(-0.3 * layer_idx)


def kernel(x, g_ffn1, ffn1_w_gu, ffn1_w_down, g_mix, w_in, b_f, rel_bias, diff_lambda, w_out,
           g_ffn2, ffn2_w_gu, ffn2_w_down, g_final):
    b, s, d = x.shape
    depth = g_ffn1.shape[0]
    x2 = x.reshape(b * s, d)
    for layer in range(depth):
        x2 = _ffn(x2, g_ffn1[layer], ffn1_w_gu[layer], ffn1_w_down[layer])
        qk, vt, kaug, qaug = _inproj(x2.reshape(b, s, d), g_mix[layer], w_in[layer], b_f[layer])
        mixed = _mixers(qk, vt, kaug, qaug, rel_bias[layer], diff_lambda[layer], _lambda_init(layer))
        x2 = _ffn(x2, g_ffn2[layer], ffn2_w_gu[layer], ffn2_w_down[layer], mixed, w_out[layer],
                  g_final if layer == depth - 1 else None)
    return x2.reshape(b, s, d)
```

```python
import functools
import math

import numpy as np
import jax
import jax.numpy as jnp
from jax import lax
from jax.experimental import pallas as pl
from jax.experimental.pallas import tpu as pltpu

F32 = jnp.float32
BF16 = jnp.bfloat16

LANES = 128
HEAD_DIM = 64
HEADS_PER_GROUP = 4
GROUP_WIDTH = HEADS_PER_GROUP * HEAD_DIM
N_GROUPS = 4
CHUNK = 64
BAND_CHUNKS = 8
MAX_REL_DIST = 256
DIFF_QK_DIM = HEAD_DIM // 2
FFN_RES = 0.5
RMS_EPS = 1e-6
NEG = -1e30
LOG2E = math.log2(math.e)
N_SPLIT = 3
ONES_ROWS = 16

TOKEN_TILE = 512
ATTN_TILE = 256
CHAIN_GROUP = 2
MXU_DIM = 256
FFN_CHUNK = 6 * MXU_DIM
PREFIX_BLOCK = MXU_DIM
VMEM_LIMIT = 56 * 1024 * 1024

_NT = (((1,), (1,)), ((), ()))


def _rmsnorm(x, g):
    y = x * lax.rsqrt(jnp.mean(x * x, axis=-1, keepdims=True) + RMS_EPS)
    return y * g


def _log_sigmoid(x):
    return jnp.minimum(x, 0.0) - jnp.log1p(jnp.exp(-jnp.abs(x)))


def _split3(x):
    h1 = x.astype(BF16)
    r1 = x - h1.astype(F32)
    h2 = r1.astype(BF16)
    h3 = (r1 - h2.astype(F32)).astype(BF16)
    return h1, h2, h3


def _split3_const(c):
    parts = []
    for _ in range(N_SPLIT):
        p = float(np.asarray(c, np.float32).astype(jnp.bfloat16).astype(np.float32))
        parts.append(p)
        c = c - p
    return parts


def _params(sem):
    return pltpu.CompilerParams(dimension_semantics=sem, vmem_limit_bytes=VMEM_LIMIT)


def _resident(shape):
    zeros = (0,) * len(shape)
    return pl.BlockSpec(shape, lambda *_: zeros, pipeline_mode=pl.Buffered(1))


def _ffn_kernel(*refs, d_ff, chunk, n_mixed, final_norm):
    x_ref, refs = refs[0], refs[1:]
    x = x_ref[...]
    if n_mixed:
        wout_ref = refs[n_mixed]
        width = wout_ref.shape[0] // n_mixed
        for grp, m_ref in enumerate(refs[:n_mixed]):
            x = x + jnp.dot(m_ref[...], wout_ref[grp * width:(grp + 1) * width, :],
                            preferred_element_type=F32)
        refs = refs[n_mixed + 1:]
    g_ref, wgu_ref, wd_ref = refs[:3]
    o_ref = refs[-1]
    h = _rmsnorm(x, g_ref[...]).astype(BF16)
    acc = jnp.zeros(x.shape, F32)
    for c0 in range(0, d_ff, chunk):
        c1 = min(c0 + chunk, d_ff)
        gate = jnp.dot(h, wgu_ref[:, c0:c1], preferred_element_type=F32)
        up = jnp.dot(h, wgu_ref[:, d_ff + c0:d_ff + c1], preferred_element_type=F32)
        act = gate * jax.nn.sigmoid(gate) * up
        acc = acc + jnp.dot(act.astype(BF16), wd_ref[c0:c1, :], preferred_element_type=F32)
    y = x + FFN_RES * acc
    o_ref[...] = _rmsnorm(y, refs[3][...]) if final_norm else y


def _ffn(x2, g, w_gu, w_down, mixed=(), w_out=None, g_final=None):
    n, d = x2.shape
    d_ff = w_down.shape[0]
    chunk = FFN_CHUNK
    tm = min(TOKEN_TILE, n)
    row_spec = pl.BlockSpec((tm, d), lambda i: (i, 0))
    args, specs = [x2], [row_spec]
    if mixed:
        for m in mixed:
            args.append(m.reshape(n, m.shape[-1]))
            specs.append(pl.BlockSpec((tm, m.shape[-1]), lambda i: (i, 0)))
        args.append(w_out.astype(BF16))
        specs.append(_resident(w_out.shape))
    args += [g.reshape(1, d), w_gu.astype(BF16), w_down.astype(BF16)]
    specs += [_resident((1, d)), _resident((d, 2 * d_ff)), _resident((d_ff, d))]
    if g_final is not None:
        args.append(g_final.reshape(1, d))
        specs.append(_resident((1, d)))
    return pl.pallas_call(
        functools.partial(_ffn_kernel, d_ff=d_ff, chunk=chunk, n_mixed=len(mixed),
                          final_norm=g_final is not None),
        grid=(n // tm,),
        in_specs=specs,
        out_specs=row_spec,
        out_shape=jax.ShapeDtypeStruct((n, d), F32),
        compiler_params=_params(("arbitrary",)),
        name="ffn",
    )(*args)


def _inproj_kernel(x_ref, g_ref, wqk_ref, cs_ref, wvt_ref, wf_ref, bf_ref,
                   qk_ref, vt_ref, kaug_ref, qaug_ref, carry_ref, *, tm, blk):
    @pl.when(pl.program_id(1) == 0)
    def _():
        carry_ref[...] = jnp.zeros_like(carry_ref)

    h = _rmsnorm(x_ref[0], g_ref[...]).astype(BF16)
    qk = jnp.dot(h, wqk_ref[...], preferred_element_type=F32)
    qk_ref[0] = (qk * cs_ref[...]).astype(BF16)
    vt_ref[0] = lax.dot_general(wvt_ref[...], h, _NT, preferred_element_type=F32).astype(BF16)

    ri = lax.broadcasted_iota(jnp.int32, (blk, blk), 0)
    ci = lax.broadcasted_iota(jnp.int32, (blk, blk), 1)
    tril = jnp.where(ci <= ri, 1.0, 0.0).astype(BF16)
    ls = _log_sigmoid(jnp.dot(h, wf_ref[...], preferred_element_type=F32) + bf_ref[...])
    lane = lax.broadcasted_iota(jnp.int32, (1, LANES), 1)
    used = lane < 2 * N_SPLIT * HEADS_PER_GROUP
    slot = lane % (2 * N_SPLIT)
    carry = carry_ref[...]
    for r0 in range(0, tm, blk):
        cum = carry
        for part in _split3(ls[r0:r0 + blk]):
            cum = cum + jnp.dot(tril, part, preferred_element_type=F32)
        carry = cum[blk - 1:blk, :]
        kaug = jnp.where(slot >= N_SPLIT, 1.0, 0.0)
        qaug = jnp.where(slot < N_SPLIT, -1.0, 0.0)
        for i, part in enumerate(_split3(cum * LOG2E)):
            kaug = jnp.where(slot == i, part.astype(F32), kaug)
            qaug = jnp.where(slot == N_SPLIT + i, part.astype(F32), qaug)
        kaug_ref[0, r0:r0 + blk, :] = jnp.where(used, kaug, 0.0).astype(BF16)
        qaug_ref[0, r0:r0 + blk, :] = jnp.where(used, qaug, 0.0).astype(BF16)
    carry_ref[...] = carry


def _inproj(x3, g, w_in, b_f):
    b, s, d = x3.shape
    gw = GROUP_WIDTH
    nf = HEADS_PER_GROUP
    starts = (0, 3 * gw + nf, 6 * gw + nf, 9 * gw + nf)
    w_qk = jnp.concatenate([w_in[:, st:st + 2 * gw] for st in starts], axis=1).astype(BF16)
    w_vt = jnp.concatenate([w_in[:, st + 2 * gw:st + 3 * gw] for st in starts], axis=1).T.astype(BF16)
    rep = 2 * N_SPLIT
    w_f = jnp.pad(jnp.repeat(w_in[:, 3 * gw:3 * gw + nf], rep, axis=1),
                  ((0, 0), (0, LANES - rep * nf))).astype(BF16)
    b_col = jnp.pad(jnp.repeat(b_f, rep), (0, LANES - rep * nf)).reshape(1, LANES)
    width = w_qk.shape[1]
    cs = np.ones((1, width), np.float32)
    for grp in range(N_GROUPS):
        scale = DIFF_QK_DIM ** -0.5 if grp == 2 else HEAD_DIM ** -0.5
        cs[0, 2 * grp * gw:(2 * grp + 1) * gw] = scale * LOG2E
    tm = min(TOKEN_TILE, s)
    blk = min(PREFIX_BLOCK, tm)
    return pl.pallas_call(
        functools.partial(_inproj_kernel, tm=tm, blk=blk),
        grid=(b, s // tm),
        in_specs=[
            pl.BlockSpec((1, tm, d), lambda bi, j: (bi, j, 0)),
            _resident((1, d)),
            _resident((d, width)),
            _resident((1, width)),
            _resident((N_GROUPS * gw, d)),
            _resident((d, LANES)),
            _resident((1, LANES)),
        ],
        out_specs=[
            pl.BlockSpec((1, tm, width), lambda bi, j: (bi, j, 0)),
            pl.BlockSpec((1, N_GROUPS * gw, tm), lambda bi, j: (bi, 0, j)),
            pl.BlockSpec((1, tm, LANES), lambda bi, j: (bi, j, 0)),
            pl.BlockSpec((1, tm, LANES), lambda bi, j: (bi, j, 0)),
        ],
        out_shape=[
            jax.ShapeDtypeStruct((b, s, width), BF16),
            jax.ShapeDtypeStruct((b, N_GROUPS * gw, s), BF16),
            jax.ShapeDtypeStruct((b, s, LANES), BF16),
            jax.ShapeDtypeStruct((b, s, LANES), BF16),
        ],
        scratch_shapes=[pltpu.VMEM((1, LANES), F32)],
        compiler_params=_params(("arbitrary", "arbitrary")),
        name="inproj",
    )(x3, g.reshape(1, d), w_qk, jnp.asarray(cs), w_vt, w_f, b_col)


def _lane_range(lo, width):
    lane = lax.broadcasted_iota(jnp.int32, (1, LANES), 1)
    return (lane >= lo) & (lane < lo + width)


def _rows(tile, t):
    return slice(tile * t, (tile + 1) * t)


def _query_rows(ref, q_tile, t):
    return ref[0, _rows(q_tile, t), :]


def _masked_queries(q, width):
    out = []
    for pair in range(2):
        q_pair = q[:, pair * LANES:(pair + 1) * LANES]
        for lo in range(0, LANES, width):
            out.append(jnp.where(_lane_range(lo, width), q_pair, jnp.zeros_like(q_pair)))
    return out


def _head_scores(k_ref, qs, t, per_head, k_extra=None):
    def scores(kb, cs):
        out = []
        for c in cs:
            pair = c // (2 * per_head)
            k = k_ref[0, _rows(kb, t), pair * LANES:(pair + 1) * LANES]
            if k_extra is not None:
                k = jnp.concatenate([k, k_extra(c, kb)], axis=1)
            out.append(lax.dot_general(k, qs[c], _NT, preferred_element_type=F32))
        return out
    return scores


def _head_products(vt_ref, t, per_head, with_ones=True):
    def products(ps, kb, cs):
        out = []
        for c, p in zip(cs, ps):
            h = c // per_head
            vt = vt_ref[0, h * HEAD_DIM:(h + 1) * HEAD_DIM, _rows(kb, t)]
            if with_ones:
                vt = jnp.concatenate([vt, jnp.ones((ONES_ROWS, t), vt.dtype)], axis=0)
            out.append(jnp.dot(vt, p, preferred_element_type=F32))
        return out
    return products


_MAX, _ALPHA, _OFFSET = range(3)


def _flash_forward(qi, n_q, t, n_chain, group, tile_ops, products, emit,
                   s_scr, p_scr, acc_scr, st_scr):
    chains = range(n_chain)
    kb_lo, kb_hi, scores, adjust, shift_of = tile_ops(qi)

    def land(kb, tiles, diag, cs=chains, adjust=adjust, shift_of=shift_of):
        slot = kb & 1
        for c, tile in zip(cs, tiles):
            s = adjust(c, tile, kb, diag)
            s_scr[slot, c] = s
            mx = jnp.max(s, axis=0, keepdims=True)
            shift = shift_of(c, kb, diag)
            if shift is not None:
                mx = mx + shift
            m_old = st_scr[_MAX, c]
            m_new = jnp.maximum(m_old, mx)
            st_scr[_MAX, c] = m_new
            st_scr[_ALPHA, c] = jnp.exp2(m_old - m_new)
            st_scr[_OFFSET, c] = m_new if shift is None else m_new - shift

    def weigh(kb, cs=chains):
        slot = kb & 1
        for c in cs:
            r = st_scr[_OFFSET, c]
            for lo in range(0, t, LANES):
                cols = slice(lo, lo + LANES)
                p_scr[c, :, cols] = jnp.exp2(s_scr[slot, c, :, cols] - r[:, cols]).astype(BF16)
        return products([p_scr[c] for c in cs], kb, cs)

    def step(kb, diag_next):
        for g0 in range(0, n_chain, group):
            cs = range(g0, min(g0 + group, n_chain))
            alpha = [st_scr[_ALPHA, c] for c in cs]
            nxt = scores(kb + 1, cs)
            prod = weigh(kb, cs)
            land(kb + 1, nxt, diag_next, cs)
            for c, a, pr in zip(cs, alpha, prod):
                acc_scr[c] = a * acc_scr[c] + pr

    def reset_stats():
        for c in chains:
            st_scr[_MAX, c] = jnp.full((1, t), NEG, F32)

    def finish():
        alpha = [st_scr[_ALPHA, c] for c in chains]
        prod = weigh(kb_hi)
        if qi + 1 < n_q:
            nxt_lo, _, nxt_scores, nxt_adjust, nxt_shift = tile_ops(qi + 1)
            reset_stats()
            land(nxt_lo, nxt_scores(nxt_lo, chains), False, chains, nxt_adjust, nxt_shift)
        emit([alpha[c] * acc_scr[c] + prod[c] for c in chains])

    for c in chains:
        acc_scr[c] = jnp.zeros(acc_scr.shape[1:], F32)
    if qi == 0:
        reset_stats()
        land(kb_lo, scores(kb_lo, chains), True)
    for kb in range(kb_lo, kb_hi):
        step(kb, kb == kb_hi - 1)
    finish()


def _normalized(acc):
    return acc[:HEAD_DIM] * (1.0 / acc[HEAD_DIM:HEAD_DIM + 1])


def _store_pair(o_ref, q_tile, pair, top, bottom):
    o_t = jnp.concatenate([top, bottom], axis=0)
    rows = _rows(q_tile, o_t.shape[1])
    o_ref[0, rows, pair * LANES:(pair + 1) * LANES] = o_t.T.astype(o_ref.dtype)


def _store_normalized(o_ref, q_tile, acc):
    for pair in range(2):
        _store_pair(o_ref, q_tile, pair, _normalized(acc[2 * pair]), _normalized(acc[2 * pair + 1]))


def _for_query_tiles(o_ref, t, tile_body):
    n_q = o_ref.shape[1] // t
    for qi in range(n_q):
        tile_body(qi, n_q)


def _tile_iotas(t):
    key = lax.broadcasted_iota(jnp.int32, (t, t), 0)
    qry = lax.broadcasted_iota(jnp.int32, (t, t), 1)
    return key, qry


def _no_shift(c, kb, diag):
    return None


def _fox_kernel(q_ref, k_ref, vt_ref, qaug_ref, kaug_ref, o_ref, s_scr, p_scr, acc_scr, st_scr,
                *, t):
    rep = 2 * N_SPLIT

    def k_extra(h, kb):
        return kaug_ref[0, _rows(kb, t), :]

    def adjust(h, s, kb, diag):
        if not diag:
            return s
        key, qry = _tile_iotas(t)
        return jnp.where(key <= qry, s, NEG)

    def tile_ops(q_tile):
        q_aug = _query_rows(qaug_ref, q_tile, t)
        qs = [jnp.concatenate(
                  [q, jnp.where(_lane_range(rep * h, rep), q_aug, jnp.zeros_like(q_aug))], axis=1)
              for h, q in enumerate(_masked_queries(_query_rows(q_ref, q_tile, t), HEAD_DIM))]
        return 0, q_tile, _head_scores(k_ref, qs, t, 1, k_extra), adjust, _no_shift

    def query_tile(qi, n_q):
        _flash_forward(qi, n_q, t, HEADS_PER_GROUP, 2 * CHAIN_GROUP,
                       tile_ops, _head_products(vt_ref, t, 1),
                       functools.partial(_store_normalized, o_ref, qi),
                       s_scr, p_scr, acc_scr, st_scr)

    _for_query_tiles(o_ref, t, query_tile)


def _band_kernel(q_ref, k_ref, vt_ref, bias_ref, o_ref, s_scr, p_scr, acc_scr, st_scr,
                 *, t, reach):
    def tile_ops(q_tile):
        qs = _masked_queries(_query_rows(q_ref, q_tile, t), HEAD_DIM)

        def adjust(h, s, kb, diag):
            return s + bias_ref[h, kb - q_tile + reach]

        return max(q_tile - reach, 0), q_tile, _head_scores(k_ref, qs, t, 1), adjust, _no_shift

    def query_tile(qi, n_q):
        _flash_forward(qi, n_q, t, HEADS_PER_GROUP, CHAIN_GROUP,
                       tile_ops, _head_products(vt_ref, t, 1),
                       functools.partial(_store_normalized, o_ref, qi),
                       s_scr, p_scr, acc_scr, st_scr)

    _for_query_tiles(o_ref, t, query_tile)


def _alibi_lanes(t):
    q_aug = np.zeros((HEADS_PER_GROUP, t, LANES), np.float32)
    k_aug = np.zeros((HEADS_PER_GROUP, t, LANES), np.float32)
    off = np.arange(t, dtype=np.float32)
    for h in range(HEADS_PER_GROUP):
        parts = _split3_const(_alibi_slope(h) * LOG2E)
        for i, c in enumerate(parts):
            q_aug[h, :, i] = off
            k_aug[h, :, i] = -c
            q_aug[h, :, N_SPLIT + i] = c
            k_aug[h, :, N_SPLIT + i] = off
    return jnp.asarray(q_aug, BF16), jnp.asarray(k_aug, BF16)


def _alibi_slope(h):
    return 2.0 ** (-8.0 * (h + 1) / HEADS_PER_GROUP)


def _alibi_diag_fix(t):
    key = np.arange(t)[:, None]
    qry = np.arange(t)[None, :]
    dist = np.minimum(qry - key, 0).astype(np.float32)
    fix = np.stack([dist * np.float32(2.0 * _alibi_slope(h) * LOG2E) for h in range(HEADS_PER_GROUP)])
    return jnp.asarray(np.where((key // CHUNK <= qry // CHUNK)[None], fix, np.float32(NEG)))


def _diff_kernel(q_ref, k_ref, vt_ref, lam_ref, qal_ref, kal_ref, fix_ref, o_ref,
                 s_scr, p_scr, acc_scr, st_scr, *, t, lam_init):
    lp = lam_ref[...]
    lam = (jnp.exp(jnp.sum(lp[0:1] * lp[1:2], axis=-1, keepdims=True))
           - jnp.exp(jnp.sum(lp[2:3] * lp[3:4], axis=-1, keepdims=True)) + lam_init)
    def k_extra(c, kb):
        return kal_ref[c // 2]

    def adjust(c, s, kb, diag):
        return s + fix_ref[c // 2] if diag else s

    def tile_ops(q_tile):
        qs = [jnp.concatenate([q, qal_ref[c // 2]], axis=1)
              for c, q in enumerate(_masked_queries(_query_rows(q_ref, q_tile, t), DIFF_QK_DIM))]

        def shift_of(c, kb, diag):
            if diag:
                return None
            return -_alibi_slope(c // 2) * LOG2E * t * (q_tile - kb)

        return 0, q_tile, _head_scores(k_ref, qs, t, 2, k_extra), adjust, shift_of

    n_chain = 2 * HEADS_PER_GROUP

    def emit(qi, acc):
        for pair in range(2):
            halves = []
            for hh in range(2):
                h = 2 * pair + hh
                o_h = _normalized(acc[2 * h]) - lam * _normalized(acc[2 * h + 1])
                ms = jnp.mean(o_h * o_h, axis=0, keepdims=True)
                halves.append(o_h * lax.rsqrt(ms + RMS_EPS) * (1.0 - lam_init))
            _store_pair(o_ref, qi, pair, halves[0], halves[1])

    def query_tile(qi, n_q):
        _flash_forward(qi, n_q, t, n_chain, 2 * CHAIN_GROUP, tile_ops,
                       _head_products(vt_ref, t, 2), functools.partial(emit, qi),
                       s_scr, p_scr, acc_scr, st_scr)

    _for_query_tiles(o_ref, t, query_tile)


def _softplus2(z):
    neg_abs = pltpu.bitcast(pltpu.bitcast(z, jnp.uint32) | jnp.uint32(0x80000000), F32)
    return jnp.maximum(z, 0.0) + jnp.log(1.0 + jnp.exp2(neg_abs)) * LOG2E


def _stick_kernel(q_ref, k_ref, vt_ref, o_ref, z_scr, w_scr, acc_scr, st_scr, wi_scr, *, t):
    _for_query_tiles(o_ref, t, lambda qi, n_q: _stick_tile(
        qi, q_ref, k_ref, vt_ref, o_ref, z_scr, w_scr, acc_scr, st_scr, wi_scr, t))


def _stick_tile(qi, q_ref, k_ref, vt_ref, o_ref, z_scr, w_scr, acc_scr, st_scr, wi_scr, t):
    chains = range(HEADS_PER_GROUP)
    products = _head_products(vt_ref, t, 1, with_ones=False)
    tail_scr = st_scr.at[0]

    scores = _head_scores(k_ref, _masked_queries(_query_rows(q_ref, qi, t), HEAD_DIM), t, 1)

    def land(j, tiles, diag, cs=chains):
        slot = j & 1
        key, qry = _tile_iotas(t)
        minus_suffix = jnp.where(qry >= key, -1.0, 0.0).astype(BF16)
        within = []
        for c, z in zip(cs, tiles):
            rest = _softplus2(z)
            if diag:
                rest = jnp.where(key < qry, rest, 0.0)
                z = jnp.where(key < qry, z, NEG)
            z_scr[slot, c] = z
            hi = rest.astype(BF16)
            lo = (rest - hi.astype(F32)).astype(BF16)
            within.append(jnp.dot(minus_suffix, hi, preferred_element_type=F32)
                          + jnp.dot(minus_suffix, lo, preferred_element_type=F32))
        return within

    def weigh(j, cs=chains):
        slot = j & 1
        for c in cs:
            tail = tail_scr[c]
            for lo in range(0, t, LANES):
                cols = slice(lo, lo + LANES)
                logw = z_scr[slot, c, :, cols] + wi_scr[c, :, cols] + tail[:, cols]
                w_scr[c, :, cols] = jnp.exp2(logw).astype(BF16)
            tail_scr[c] = tail + wi_scr[c, 0:1, :]
        return products([w_scr[c] for c in cs], qi - j, cs)

    def trip(j):
        nxt = scores(qi - j - 1, chains)
        prod = weigh(j)
        within = land(j + 1, nxt, False)
        for c in chains:
            acc_scr[c] = acc_scr[c] + prod[c]
            wi_scr[c] = within[c]

    for c, wi in enumerate(land(0, scores(qi, chains), True)):
        tail_scr[c] = jnp.zeros((1, t), F32)
        acc_scr[c] = jnp.zeros(acc_scr.shape[1:], F32)
        wi_scr[c] = wi
    for j in range(qi):
        trip(j)
    prod = weigh(qi)
    acc = [acc_scr[c] + prod[c] for c in chains]
    for pair in range(2):
        _store_pair(o_ref, qi, pair, acc[2 * pair], acc[2 * pair + 1])


def _mixer_call(body, qk, vt, group, n_chain, extra_inputs, extra_specs, name,
                acc_rows=HEAD_DIM + ONES_ROWS, extra_scratch=()):
    b, s, _ = qk.shape
    t = min(ATTN_TILE, s)
    gw = GROUP_WIDTH
    return pl.pallas_call(
        functools.partial(body, t=t),
        grid=(b,),
        in_specs=[
            pl.BlockSpec((1, s, gw), lambda bi: (bi, 0, 2 * group)),
            pl.BlockSpec((1, s, gw), lambda bi: (bi, 0, 2 * group + 1)),
            pl.BlockSpec((1, gw, s), lambda bi: (bi, group, 0)),
        ] + extra_specs,
        out_specs=pl.BlockSpec((1, s, gw), lambda bi: (bi, 0, 0)),
        out_shape=jax.ShapeDtypeStruct((b, s, gw), BF16),
        scratch_shapes=[pltpu.VMEM((2, n_chain, t, t), F32),
                        pltpu.VMEM((n_chain, t, t), BF16),
                        pltpu.VMEM((n_chain, acc_rows, t), F32),
                        pltpu.VMEM((3, n_chain, 1, t), F32)]
                       + list(extra_scratch),
        compiler_params=_params(("arbitrary",)),
        name=name,
    )(qk, qk, vt, *extra_inputs)


def _band_bias(rel_table, t, reach):
    n_keys = (reach + 1) * t
    period = n_keys + t
    d = np.arange(period)
    d = np.where(d < t, d, d - period)
    idx = np.clip(reach * t + d, -MAX_REL_DIST, MAX_REL_DIST) + MAX_REL_DIST
    ext = rel_table.astype(F32)[:, idx] * LOG2E
    h = rel_table.shape[0]
    rows = jnp.tile(ext, (1, n_keys))[:, :n_keys * (period - 1)].reshape(h, n_keys, period - 1)
    bias = rows[:, :, :t]
    kc = (np.arange(n_keys) // CHUNK - (reach * t // CHUNK - BAND_CHUNKS))[:, None]
    qc = (np.arange(t) // CHUNK)[None, :]
    visible = (kc >= qc) & (kc <= qc + BAND_CHUNKS)
    return jnp.where(visible[None], bias, NEG).reshape(h, reach + 1, t, t)


def _mixers(qk, vt, kaug, qaug, rel_table, lam_params, lam_init):
    b, s, _ = qk.shape
    t = min(ATTN_TILE, s)
    assert s % t == 0 and t % CHUNK == 0 and t % LANES == 0
    reach = -(-BAND_CHUNKS * CHUNK // t)
    nh = HEADS_PER_GROUP
    o_a = _mixer_call(
        _fox_kernel, qk, vt, 0, nh, [qaug, kaug],
        [pl.BlockSpec((1, s, LANES), lambda bi: (bi, 0, 0)),
         pl.BlockSpec((1, s, LANES), lambda bi: (bi, 0, 0))], "fox")
    bias = _band_bias(rel_table, t, reach)
    o_b = _mixer_call(
        functools.partial(_band_kernel, reach=reach), qk, vt, 1, nh, [bias],
        [_resident(bias.shape)], "band")
    q_al, k_al = _alibi_lanes(t)
    fix = _alibi_diag_fix(t)
    o_c = _mixer_call(
        functools.partial(_diff_kernel, lam_init=lam_init), qk, vt, 2, 2 * nh,
        [lam_params, q_al, k_al, fix],
        [_resident(lam_params.shape), _resident(q_al.shape), _resident(k_al.shape),
         _resident(fix.shape)], "diff")
    o_d = _mixer_call(_stick_kernel, qk, vt, 3, nh, [], [], "stick", acc_rows=HEAD_DIM,
                      extra_scratch=[pltpu.VMEM((nh, t, t), F32)])
    return o_a, o_b, o_c, o_d


def _lambda_init(layer_idx):
    return 0.8 - 0.6 * math.exp(-0.3 * layer_idx)


def kernel(x, g_ffn1, ffn1_w_gu, ffn1_w_down, g_mix, w_in, b_f, rel_bias, diff_lambda, w_out,
           g_ffn2, ffn2_w_gu, ffn2_w_down, g_final):
    b, s, d = x.shape
    depth = g_ffn1.shape[0]
    x2 = x.reshape(b * s, d)
    for layer in range(depth):
        x2 = _ffn(x2, g_ffn1[layer], ffn1_w_gu[layer], ffn1_w_down[layer])
        qk, vt, kaug, qaug = _inproj(x2.reshape(b, s, d), g_mix[layer], w_in[layer], b_f[layer])
        mixed = _mixers(qk, vt, kaug, qaug, rel_bias[layer], diff_lambda[layer], _lambda_init(layer))
        x2 = _ffn(x2, g_ffn2[layer], ffn2_w_gu[layer], ffn2_w_down[layer], mixed, w_out[layer],
                  g_final if layer == depth - 1 else None)
    return x2.reshape(b, s, d)
```

```python
import functools
import math

import numpy as np
import jax
import jax.numpy as jnp
from jax import lax
from jax.experimental import pallas as pl
from jax.experimental.pallas import tpu as pltpu

F32 = jnp.float32
BF16 = jnp.bfloat16

LANES = 128
HEAD_DIM = 64
HEADS_PER_GROUP = 4
GROUP_WIDTH = HEADS_PER_GROUP * HEAD_DIM
N_GROUPS = 4
CHUNK = 64
BAND_CHUNKS = 8
MAX_REL_DIST = 256
DIFF_QK_DIM = HEAD_DIM // 2
FFN_RES = 0.5
RMS_EPS = 1e-6
NEG = -1e30
LOG2E = math.log2(math.e)
N_SPLIT = 3
ONES_ROWS = 16

TOKEN_TILE = 512
ATTN_TILE = 256
CHAIN_GROUP = 2
MXU_DIM = 256
FFN_CHUNK = 6 * MXU_DIM
PREFIX_BLOCK = MXU_DIM
VMEM_LIMIT = 56 * 1024 * 1024

_NT = (((1,), (1,)), ((), ()))


def _rmsnorm(x, g):
    y = x * lax.rsqrt(jnp.mean(x * x, axis=-1, keepdims=True) + RMS_EPS)
    return y * g


def _log_sigmoid(x):
    return jnp.minimum(x, 0.0) - jnp.log1p(jnp.exp(-jnp.abs(x)))


def _split3(x):
    h1 = x.astype(BF16)
    r1 = x - h1.astype(F32)
    h2 = r1.astype(BF16)
    h3 = (r1 - h2.astype(F32)).astype(BF16)
    return h1, h2, h3


def _split3_const(c):
    parts = []
    for _ in range(N_SPLIT):
        p = float(np.asarray(c, np.float32).astype(jnp.bfloat16).astype(np.float32))
        parts.append(p)
        c = c - p
    return parts


def _params(sem):
    return pltpu.CompilerParams(dimension_semantics=sem, vmem_limit_bytes=VMEM_LIMIT)


def _resident(shape):
    zeros = (0,) * len(shape)
    return pl.BlockSpec(shape, lambda *_: zeros, pipeline_mode=pl.Buffered(1))


def _ffn_kernel(*refs, d_ff, chunk, n_mixed, final_norm):
    x_ref, refs = refs[0], refs[1:]
    x = x_ref[...]
    if n_mixed:
        wout_ref = refs[n_mixed]
        width = wout_ref.shape[0] // n_mixed
        for grp, m_ref in enumerate(refs[:n_mixed]):
            x = x + jnp.dot(m_ref[...], wout_ref[grp * width:(grp + 1) * width, :],
                            preferred_element_type=F32)
        refs = refs[n_mixed + 1:]
    g_ref, wgu_ref, wd_ref = refs[:3]
    o_ref = refs[-1]
    h = _rmsnorm(x, g_ref[...]).astype(BF16)
    acc = jnp.zeros(x.shape, F32)
    for c0 in range(0, d_ff, chunk):
        c1 = min(c0 + chunk, d_ff)
        gate = jnp.dot(h, wgu_ref[:, c0:c1], preferred_element_type=F32)
        up = jnp.dot(h, wgu_ref[:, d_ff + c0:d_ff + c1], preferred_element_type=F32)
        act = gate * jax.nn.sigmoid(gate) * up
        acc = acc + jnp.dot(act.astype(BF16), wd_ref[c0:c1, :], preferred_element_type=F32)
    y = x + FFN_RES * acc
    o_ref[...] = _rmsnorm(y, refs[3][...]) if final_norm else y


def _ffn(x2, g, w_gu, w_down, mixed=(), w_out=None, g_final=None):
    n, d = x2.shape
    d_ff = w_down.shape[0]
    chunk = FFN_CHUNK
    tm = min(TOKEN_TILE, n)
    row_spec = pl.BlockSpec((tm, d), lambda i: (i, 0))
    args, specs = [x2], [row_spec]
    if mixed:
        for m in mixed:
            args.append(m.reshape(n, m.shape[-1]))
            specs.append(pl.BlockSpec((tm, m.shape[-1]), lambda i: (i, 0)))
        args.append(w_out.astype(BF16))
        specs.append(_resident(w_out.shape))
    args += [g.reshape(1, d), w_gu.astype(BF16), w_down.astype(BF16)]
    specs += [_resident((1, d)), _resident((d, 2 * d_ff)), _resident((d_ff, d))]
    if g_final is not None:
        args.append(g_final.reshape(1, d))
        specs.append(_resident((1, d)))
    return pl.pallas_call(
        functools.partial(_ffn_kernel, d_ff=d_ff, chunk=chunk, n_mixed=len(mixed),
                          final_norm=g_final is not None),
        grid=(n // tm,),
        in_specs=specs,
        out_specs=row_spec,
        out_shape=jax.ShapeDtypeStruct((n, d), F32),
        compiler_params=_params(("arbitrary",)),
        name="ffn",
    )(*args)


def _inproj_kernel(x_ref, g_ref, wqk_ref, cs_ref, wvt_ref, wf_ref, bf_ref,
                   qk_ref, vt_ref, kaug_ref, qaug_ref, carry_ref, *, tm, blk):
    @pl.when(pl.program_id(1) == 0)
    def _():
        carry_ref[...] = jnp.zeros_like(carry_ref)

    h = _rmsnorm(x_ref[0], g_ref[...]).astype(BF16)
    qk = jnp.dot(h, wqk_ref[...], preferred_element_type=F32)
    qk_ref[0] = (qk * cs_ref[...]).astype(BF16)
    vt_ref[0] = lax.dot_general(wvt_ref[...], h, _NT, preferred_element_type=F32).astype(BF16)

    ri = lax.broadcasted_iota(jnp.int32, (blk, blk), 0)
    ci = lax.broadcasted_iota(jnp.int32, (blk, blk), 1)
    tril = jnp.where(ci <= ri, 1.0, 0.0).astype(BF16)
    ls = _log_sigmoid(jnp.dot(h, wf_ref[...], preferred_element_type=F32) + bf_ref[...])
    lane = lax.broadcasted_iota(jnp.int32, (1, LANES), 1)
    used = lane < 2 * N_SPLIT * HEADS_PER_GROUP
    slot = lane % (2 * N_SPLIT)
    carry = carry_ref[...]
    for r0 in range(0, tm, blk):
        cum = carry
        for part in _split3(ls[r0:r0 + blk]):
            cum = cum + jnp.dot(tril, part, preferred_element_type=F32)
        carry = cum[blk - 1:blk, :]
        kaug = jnp.where(slot >= N_SPLIT, 1.0, 0.0)
        qaug = jnp.where(slot < N_SPLIT, -1.0, 0.0)
        for i, part in enumerate(_split3(cum * LOG2E)):
            kaug = jnp.where(slot == i, part.astype(F32), kaug)
            qaug = jnp.where(slot == N_SPLIT + i, part.astype(F32), qaug)
        kaug_ref[0, r0:r0 + blk, :] = jnp.where(used, kaug, 0.0).astype(BF16)
        qaug_ref[0, r0:r0 + blk, :] = jnp.where(used, qaug, 0.0).astype(BF16)
    carry_ref[...] = carry


def _inproj(x3, g, w_in, b_f):
    b, s, d = x3.shape
    gw = GROUP_WIDTH
    nf = HEADS_PER_GROUP
    starts = (0, 3 * gw + nf, 6 * gw + nf, 9 * gw + nf)
    w_qk = jnp.concatenate([w_in[:, st:st + 2 * gw] for st in starts], axis=1).astype(BF16)
    w_vt = jnp.concatenate([w_in[:, st + 2 * gw:st + 3 * gw] for st in starts], axis=1).T.astype(BF16)
    rep = 2 * N_SPLIT
    w_f = jnp.pad(jnp.repeat(w_in[:, 3 * gw:3 * gw + nf], rep, axis=1),
                  ((0, 0), (0, LANES - rep * nf))).astype(BF16)
    b_col = jnp.pad(jnp.repeat(b_f, rep), (0, LANES - rep * nf)).reshape(1, LANES)
    width = w_qk.shape[1]
    cs = np.ones((1, width), np.float32)
    for grp in range(N_GROUPS):
        scale = DIFF_QK_DIM ** -0.5 if grp == 2 else HEAD_DIM ** -0.5
        cs[0, 2 * grp * gw:(2 * grp + 1) * gw] = scale * LOG2E
    tm = min(TOKEN_TILE, s)
    blk = min(PREFIX_BLOCK, tm)
    return pl.pallas_call(
        functools.partial(_inproj_kernel, tm=tm, blk=blk),
        grid=(b, s // tm),
        in_specs=[
            pl.BlockSpec((1, tm, d), lambda bi, j: (bi, j, 0)),
            _resident((1, d)),
            _resident((d, width)),
            _resident((1, width)),
            _resident((N_GROUPS * gw, d)),
            _resident((d, LANES)),
            _resident((1, LANES)),
        ],
        out_specs=[
            pl.BlockSpec((1, tm, width), lambda bi, j: (bi, j, 0)),
            pl.BlockSpec((1, N_GROUPS * gw, tm), lambda bi, j: (bi, 0, j)),
            pl.BlockSpec((1, tm, LANES), lambda bi, j: (bi, j, 0)),
            pl.BlockSpec((1, tm, LANES), lambda bi, j: (bi, j, 0)),
        ],
        out_shape=[
            jax.ShapeDtypeStruct((b, s, width), BF16),
            jax.ShapeDtypeStruct((b, N_GROUPS * gw, s), BF16),
            jax.ShapeDtypeStruct((b, s, LANES), BF16),
            jax.ShapeDtypeStruct((b, s, LANES), BF16),
        ],
        scratch_shapes=[pltpu.VMEM((1, LANES), F32)],
        compiler_params=_params(("arbitrary", "arbitrary")),
        name="inproj",
    )(x3, g.reshape(1, d), w_qk, jnp.asarray(cs), w_vt, w_f, b_col)


def _lane_range(lo, width):
    lane = lax.broadcasted_iota(jnp.int32, (1, LANES), 1)
    return (lane >= lo) & (lane < lo + width)


def _rows(tile, t):
    return slice(tile * t, (tile + 1) * t)


def _query_rows(ref, q_tile, t):
    return ref[0, _rows(q_tile, t), :]


def _masked_queries(q, width):
    out = []
    for pair in range(2):
        q_pair = q[:, pair * LANES:(pair + 1) * LANES]
        for lo in range(0, LANES, width):
            out.append(jnp.where(_lane_range(lo, width), q_pair, jnp.zeros_like(q_pair)))
    return out


def _head_scores(k_ref, qs, t, per_head, k_extra=None):
    def scores(kb, cs):
        out = []
        for c in cs:
            pair = c // (2 * per_head)
            k = k_ref[0, _rows(kb, t), pair * LANES:(pair + 1) * LANES]
            if k_extra is not None:
                k = jnp.concatenate([k, k_extra(c, kb)], axis=1)
            out.append(lax.dot_general(k, qs[c], _NT, preferred_element_type=F32))
        return out
    return scores


def _head_products(vt_ref, t, per_head, with_ones=True):
    def products(ps, kb, cs):
        out = []
        for c, p in zip(cs, ps):
            h = c // per_head
            vt = vt_ref[0, h * HEAD_DIM:(h + 1) * HEAD_DIM, _rows(kb, t)]
            if with_ones:
                vt = jnp.concatenate([vt, jnp.ones((ONES_ROWS, t), vt.dtype)], axis=0)
            out.append(jnp.dot(vt, p, preferred_element_type=F32))
        return out
    return products


_MAX, _ALPHA, _OFFSET = range(3)


def _flash_forward(qi, n_q, t, n_chain, group, tile_ops, products, emit,
                   s_scr, p_scr, acc_scr, st_scr):
    chains = range(n_chain)
    kb_lo, kb_hi, scores, adjust, shift_of = tile_ops(qi)

    def land(kb, tiles, diag, cs=chains, adjust=adjust, shift_of=shift_of):
        slot = kb & 1
        for c, tile in zip(cs, tiles):
            s = adjust(c, tile, kb, diag)
            s_scr[slot, c] = s
            mx = jnp.max(s, axis=0, keepdims=True)
            shift = shift_of(c, kb, diag)
            if shift is not None:
                mx = mx + shift
            m_old = st_scr[_MAX, c]
            m_new = jnp.maximum(m_old, mx)
            st_scr[_MAX, c] = m_new
            st_scr[_ALPHA, c] = jnp.exp2(m_old - m_new)
            st_scr[_OFFSET, c] = m_new if shift is None else m_new - shift

    def weigh(kb, cs=chains):
        slot = kb & 1
        for c in cs:
            r = st_scr[_OFFSET, c]
            for lo in range(0, t, LANES):
                cols = slice(lo, lo + LANES)
                p_scr[c, :, cols] = jnp.exp2((s_scr[slot, c, :, cols] - r[:, cols]).astype(BF16))
        return products([p_scr[c] for c in cs], kb, cs)

    def step(kb, diag_next):
        for g0 in range(0, n_chain, group):
            cs = range(g0, min(g0 + group, n_chain))
            alpha = [st_scr[_ALPHA, c] for c in cs]
            nxt = scores(kb + 1, cs)
            prod = weigh(kb, cs)
            land(kb + 1, nxt, diag_next, cs)
            for c, a, pr in zip(cs, alpha, prod):
                acc_scr[c] = a * acc_scr[c] + pr

    def reset_stats():
        for c in chains:
            st_scr[_MAX, c] = jnp.full((1, t), NEG, F32)

    def finish():
        alpha = [st_scr[_ALPHA, c] for c in chains]
        prod = weigh(kb_hi)
        if qi + 1 < n_q:
            nxt_lo, _, nxt_scores, nxt_adjust, nxt_shift = tile_ops(qi + 1)
            reset_stats()
            land(nxt_lo, nxt_scores(nxt_lo, chains), False, chains, nxt_adjust, nxt_shift)
        emit([alpha[c] * acc_scr[c] + prod[c] for c in chains])

    for c in chains:
        acc_scr[c] = jnp.zeros(acc_scr.shape[1:], F32)
    if qi == 0:
        reset_stats()
        land(kb_lo, scores(kb_lo, chains), True)
    for kb in range(kb_lo, kb_hi):
        step(kb, kb == kb_hi - 1)
    finish()


def _normalized(acc):
    return acc[:HEAD_DIM] * (1.0 / acc[HEAD_DIM:HEAD_DIM + 1])


def _store_pair(o_ref, q_tile, pair, top, bottom):
    o_t = jnp.concatenate([top, bottom], axis=0)
    rows = _rows(q_tile, o_t.shape[1])
    o_ref[0, rows, pair * LANES:(pair + 1) * LANES] = o_t.T.astype(o_ref.dtype)


def _store_normalized(o_ref, q_tile, acc):
    for pair in range(2):
        _store_pair(o_ref, q_tile, pair, _normalized(acc[2 * pair]), _normalized(acc[2 * pair + 1]))


def _for_query_tiles(o_ref, t, tile_body):
    n_q = o_ref.shape[1] // t
    for qi in range(n_q):
        tile_body(qi, n_q)


def _tile_iotas(t):
    key = lax.broadcasted_iota(jnp.int32, (t, t), 0)
    qry = lax.broadcasted_iota(jnp.int32, (t, t), 1)
    return key, qry


def _no_shift(c, kb, diag):
    return None


def _fox_kernel(q_ref, k_ref, vt_ref, qaug_ref, kaug_ref, o_ref, s_scr, p_scr, acc_scr, st_scr,
                *, t):
    rep = 2 * N_SPLIT

    def k_extra(h, kb):
        return kaug_ref[0, _rows(kb, t), :]

    def adjust(h, s, kb, diag):
        if not diag:
            return s
        key, qry = _tile_iotas(t)
        return jnp.where(key <= qry, s, NEG)

    def tile_ops(q_tile):
        q_aug = _query_rows(qaug_ref, q_tile, t)
        qs = [jnp.concatenate(
                  [q, jnp.where(_lane_range(rep * h, rep), q_aug, jnp.zeros_like(q_aug))], axis=1)
              for h, q in enumerate(_masked_queries(_query_rows(q_ref, q_tile, t), HEAD_DIM))]
        return 0, q_tile, _head_scores(k_ref, qs, t, 1, k_extra), adjust, _no_shift

    def query_tile(qi, n_q):
        _flash_forward(qi, n_q, t, HEADS_PER_GROUP, 2 * CHAIN_GROUP,
                       tile_ops, _head_products(vt_ref, t, 1),
                       functools.partial(_store_normalized, o_ref, qi),
                       s_scr, p_scr, acc_scr, st_scr)

    _for_query_tiles(o_ref, t, query_tile)


def _band_kernel(q_ref, k_ref, vt_ref, bias_ref, o_ref, s_scr, p_scr, acc_scr, st_scr,
                 *, t, reach):
    def tile_ops(q_tile):
        qs = _masked_queries(_query_rows(q_ref, q_tile, t), HEAD_DIM)

        def adjust(h, s, kb, diag):
            return s + bias_ref[h, kb - q_tile + reach]

        return max(q_tile - reach, 0), q_tile, _head_scores(k_ref, qs, t, 1), adjust, _no_shift

    def query_tile(qi, n_q):
        _flash_forward(qi, n_q, t, HEADS_PER_GROUP, CHAIN_GROUP,
                       tile_ops, _head_products(vt_ref, t, 1),
                       functools.partial(_store_normalized, o_ref, qi),
                       s_scr, p_scr, acc_scr, st_scr)

    _for_query_tiles(o_ref, t, query_tile)


def _alibi_lanes(t):
    q_aug = np.zeros((HEADS_PER_GROUP, t, LANES), np.float32)
    k_aug = np.zeros((HEADS_PER_GROUP, t, LANES), np.float32)
    off = np.arange(t, dtype=np.float32)
    for h in range(HEADS_PER_GROUP):
        parts = _split3_const(_alibi_slope(h) * LOG2E)
        for i, c in enumerate(parts):
            q_aug[h, :, i] = off
            k_aug[h, :, i] = -c
            q_aug[h, :, N_SPLIT + i] = c
            k_aug[h, :, N_SPLIT + i] = off
    return jnp.asarray(q_aug, BF16), jnp.asarray(k_aug, BF16)


def _alibi_slope(h):
    return 2.0 ** (-8.0 * (h + 1) / HEADS_PER_GROUP)


def _alibi_diag_fix(t):
    key = np.arange(t)[:, None]
    qry = np.arange(t)[None, :]
    dist = np.minimum(qry - key, 0).astype(np.float32)
    fix = np.stack([dist * np.float32(2.0 * _alibi_slope(h) * LOG2E) for h in range(HEADS_PER_GROUP)])
    return jnp.asarray(np.where((key // CHUNK <= qry // CHUNK)[None], fix, np.float32(NEG)))


def _diff_kernel(q_ref, k_ref, vt_ref, lam_ref, qal_ref, kal_ref, fix_ref, o_ref,
                 s_scr, p_scr, acc_scr, st_scr, *, t, lam_init):
    lp = lam_ref[...]
    lam = (jnp.exp(jnp.sum(lp[0:1] * lp[1:2], axis=-1, keepdims=True))
           - jnp.exp(jnp.sum(lp[2:3] * lp[3:4], axis=-1, keepdims=True)) + lam_init)
    def k_extra(c, kb):
        return kal_ref[c // 2]

    def adjust(c, s, kb, diag):
        return s + fix_ref[c // 2] if diag else s

    def tile_ops(q_tile):
        qs = [jnp.concatenate([q, qal_ref[c // 2]], axis=1)
              for c, q in enumerate(_masked_queries(_query_rows(q_ref, q_tile, t), DIFF_QK_DIM))]

        def shift_of(c, kb, diag):
            if diag:
                return None
            return -_alibi_slope(c // 2) * LOG2E * t * (q_tile - kb)

        return 0, q_tile, _head_scores(k_ref, qs, t, 2, k_extra), adjust, shift_of

    n_chain = 2 * HEADS_PER_GROUP

    def emit(qi, acc):
        for pair in range(2):
            halves = []
            for hh in range(2):
                h = 2 * pair + hh
                o_h = _normalized(acc[2 * h]) - lam * _normalized(acc[2 * h + 1])
                ms = jnp.mean(o_h * o_h, axis=0, keepdims=True)
                halves.append(o_h * lax.rsqrt(ms + RMS_EPS) * (1.0 - lam_init))
            _store_pair(o_ref, qi, pair, halves[0], halves[1])

    def query_tile(qi, n_q):
        _flash_forward(qi, n_q, t, n_chain, 2 * CHAIN_GROUP, tile_ops,
                       _head_products(vt_ref, t, 2), functools.partial(emit, qi),
                       s_scr, p_scr, acc_scr, st_scr)

    _for_query_tiles(o_ref, t, query_tile)


def _softplus2(z):
    neg_abs = pltpu.bitcast(pltpu.bitcast(z, jnp.uint32) | jnp.uint32(0x80000000), F32)
    return jnp.maximum(z, 0.0) + jnp.log(1.0 + jnp.exp2(neg_abs)) * LOG2E


def _stick_kernel(q_ref, k_ref, vt_ref, o_ref, z_scr, w_scr, acc_scr, st_scr, wi_scr, *, t):
    _for_query_tiles(o_ref, t, lambda qi, n_q: _stick_tile(
        qi, q_ref, k_ref, vt_ref, o_ref, z_scr, w_scr, acc_scr, st_scr, wi_scr, t))


def _stick_tile(qi, q_ref, k_ref, vt_ref, o_ref, z_scr, w_scr, acc_scr, st_scr, wi_scr, t):
    chains = range(HEADS_PER_GROUP)
    products = _head_products(vt_ref, t, 1, with_ones=False)
    tail_scr = st_scr.at[0]

    scores = _head_scores(k_ref, _masked_queries(_query_rows(q_ref, qi, t), HEAD_DIM), t, 1)

    def land(j, tiles, diag, cs=chains):
        slot = j & 1
        key, qry = _tile_iotas(t)
        minus_suffix = jnp.where(qry >= key, -1.0, 0.0).astype(BF16)
        within = []
        for c, z in zip(cs, tiles):
            rest = _softplus2(z)
            if diag:
                rest = jnp.where(key < qry, rest, 0.0)
                z = jnp.where(key < qry, z, NEG)
            z_scr[slot, c] = z
            hi = rest.astype(BF16)
            lo = (rest - hi.astype(F32)).astype(BF16)
            within.append(jnp.dot(minus_suffix, hi, preferred_element_type=F32)
                          + jnp.dot(minus_suffix, lo, preferred_element_type=F32))
        return within

    def weigh(j, cs=chains):
        slot = j & 1
        for c in cs:
            tail = tail_scr[c]
            for lo in range(0, t, LANES):
                cols = slice(lo, lo + LANES)
                logw = z_scr[slot, c, :, cols] + wi_scr[c, :, cols] + tail[:, cols]
                w_scr[c, :, cols] = jnp.exp2(logw).astype(BF16)
            tail_scr[c] = tail + wi_scr[c, 0:1, :]
        return products([w_scr[c] for c in cs], qi - j, cs)

    def trip(j):
        nxt = scores(qi - j - 1, chains)
        prod = weigh(j)
        within = land(j + 1, nxt, False)
        for c in chains:
            acc_scr[c] = acc_scr[c] + prod[c]
            wi_scr[c] = within[c]

    for c, wi in enumerate(land(0, scores(qi, chains), True)):
        tail_scr[c] = jnp.zeros((1, t), F32)
        acc_scr[c] = jnp.zeros(acc_scr.shape[1:], F32)
        wi_scr[c] = wi
    for j in range(qi):
        trip(j)
    prod = weigh(qi)
    acc = [acc_scr[c] + prod[c] for c in chains]
    for pair in range(2):
        _store_pair(o_ref, qi, pair, acc[2 * pair], acc[2 * pair + 1])


def _mixer_call(body, qk, vt, group, n_chain, extra_inputs, extra_specs, name,
                acc_rows=HEAD_DIM + ONES_ROWS, extra_scratch=()):
    b, s, _ = qk.shape
    t = min(ATTN_TILE, s)
    gw = GROUP_WIDTH
    return pl.pallas_call(
        functools.partial(body, t=t),
        grid=(b,),
        in_specs=[
            pl.BlockSpec((1, s, gw), lambda bi: (bi, 0, 2 * group)),
            pl.BlockSpec((1, s, gw), lambda bi: (bi, 0, 2 * group + 1)),
            pl.BlockSpec((1, gw, s), lambda bi: (bi, group, 0)),
        ] + extra_specs,
        out_specs=pl.BlockSpec((1, s, gw), lambda bi: (bi, 0, 0)),
        out_shape=jax.ShapeDtypeStruct((b, s, gw), BF16),
        scratch_shapes=[pltpu.VMEM((2, n_chain, t, t), F32),
                        pltpu.VMEM((n_chain, t, t), BF16),
                        pltpu.VMEM((n_chain, acc_rows, t), F32),
                        pltpu.VMEM((3, n_chain, 1, t), F32)]
                       + list(extra_scratch),
        compiler_params=_params(("arbitrary",)),
        name=name,
    )(qk, qk, vt, *extra_inputs)


def _band_bias(rel_table, t, reach):
    n_keys = (reach + 1) * t
    period = n_keys + t
    d = np.arange(period)
    d = np.where(d < t, d, d - period)
    idx = np.clip(reach * t + d, -MAX_REL_DIST, MAX_REL_DIST) + MAX_REL_DIST
    ext = rel_table.astype(F32)[:, idx] * LOG2E
    h = rel_table.shape[0]
    rows = jnp.tile(ext, (1, n_keys))[:, :n_keys * (period - 1)].reshape(h, n_keys, period - 1)
    bias = rows[:, :, :t]
    kc = (np.arange(n_keys) // CHUNK - (reach * t // CHUNK - BAND_CHUNKS))[:, None]
    qc = (np.arange(t) // CHUNK)[None, :]
    visible = (kc >= qc) & (kc <= qc + BAND_CHUNKS)
    return jnp.where(visible[None], bias, NEG).reshape(h, reach + 1, t, t)


def _mixers(qk, vt, kaug, qaug, rel_table, lam_params, lam_init):
    b, s, _ = qk.shape
    t = min(ATTN_TILE, s)
    assert s % t == 0 and t % CHUNK == 0 and t % LANES == 0
    reach = -(-BAND_CHUNKS * CHUNK // t)
    nh = HEADS_PER_GROUP
    o_a = _mixer_call(
        _fox_kernel, qk, vt, 0, nh, [qaug, kaug],
        [pl.BlockSpec((1, s, LANES), lambda bi: (bi, 0, 0)),
         pl.BlockSpec((1, s, LANES), lambda bi: (bi, 0, 0))], "fox")
    bias = _band_bias(rel_table, t, reach)
    o_b = _mixer_call(
        functools.partial(_band_kernel, reach=reach), qk, vt, 1, nh, [bias],
        [_resident(bias.shape)], "band")
    q_al, k_al = _alibi_lanes(t)
    fix = _alibi_diag_fix(t)
    o_c = _mixer_call(
        functools.partial(_diff_kernel, lam_init=lam_init), qk, vt, 2, 2 * nh,
        [lam_params, q_al, k_al, fix],
        [_resident(lam_params.shape), _resident(q_al.shape), _resident(k_al.shape),
         _resident(fix.shape)], "diff")
    o_d = _mixer_call(_stick_kernel, qk, vt, 3, nh, [], [], "stick", acc_rows=HEAD_DIM,
                      extra_scratch=[pltpu.VMEM((nh, t, t), F32)])
    return o_a, o_b, o_c, o_d


def _lambda_init(layer_idx):
    return 0.8 - 0.6 * math.exp(-0.3 * layer_idx)


def kernel(x, g_ffn1, ffn1_w_gu, ffn1_w_down, g_mix, w_in, b_f, rel_bias, diff_lambda, w_out,
           g_ffn2, ffn2_w_gu, ffn2_w_down, g_final):
    b, s, d = x.shape
    depth = g_ffn1.shape[0]
    x2 = x.reshape(b * s, d)
    for layer in range(depth):
        x2 = _ffn(x2, g_ffn1[layer], ffn1_w_gu[layer], ffn1_w_down[layer])
        qk, vt, kaug, qaug = _inproj(x2.reshape(b, s, d), g_mix[layer], w_in[layer], b_f[layer])
        mixed = _mixers(qk, vt, kaug, qaug, rel_bias[layer], diff_lambda[layer], _lambda_init(layer))
        x2 = _ffn(x2, g_ffn2[layer], ffn2_w_gu[layer], ffn2_w_down[layer], mixed, w_out[layer],
                  g_final if layer == depth - 1 else None)
    return x2.reshape(b, s, d)
```
